```python
import math
import jax, jax.numpy as jnp
from jax import lax
import numpy as np

D_MODEL = 1024
BATCH = 16
SEQ = 2048
DEPTH = 1
DEC_BATCH = 2
DEC_SEQ = 16384
PAST_LEN = 128

GRID_W = 64
NA_HEADS = 8
NA_HEAD_DIM = 64
NA_WIDTH = NA_HEADS * NA_HEAD_DIM
NA_ROWS = 8
NA_COLS = 16
NA_QBLOCK = 16
NA_SPAN = 32
N_COL_BLOCKS = GRID_W // NA_QBLOCK
GLA_HEADS = 4
GLA_DK = 128
GLA_DV = 128
GLA_KDIM = GLA_HEADS * GLA_DK
GLA_VDIM = GLA_HEADS * GLA_DV
GLA_GATE_RANK = 16
GLA_GATE_NORM = 16.0
GLA_CHUNK = 64
D_FF = 2816
CONV_WIDTH = 3
EPS = 1e-6
IN_SIZES = (NA_WIDTH, NA_WIDTH, NA_WIDTH, GLA_KDIM, GLA_KDIM, GLA_VDIM, GLA_VDIM,
            GLA_GATE_RANK, GLA_GATE_RANK, D_MODEL, D_MODEL)
IN_COLS = sum(IN_SIZES)
IN_SPLITS = tuple(int(s) for s in np.cumsum(IN_SIZES)[:-1])

kernel_name = "hybrid_natten_bigla_convffn_encoder"


def rmsnorm(x, w):
    xf = x.astype(jnp.float32)
    y = xf * lax.rsqrt(jnp.mean(xf * xf, axis=-1, keepdims=True) + EPS) * w.astype(jnp.float32)
    return y.astype(x.dtype)


def head_rmsnorm_f32(x, w):
    xf = x.astype(jnp.float32)
    return xf * lax.rsqrt(jnp.mean(xf * xf, axis=-1, keepdims=True) + EPS) * w.astype(jnp.float32)


def na_column_tables():
    col = np.arange(GRID_W)
    c0 = np.clip(col - NA_COLS // 2, 0, GRID_W - NA_COLS)
    blk_start = np.clip(np.arange(N_COL_BLOCKS) * NA_QBLOCK - NA_COLS // 2, 0, GRID_W - NA_SPAN)
    key_cols = blk_start[:, None] + np.arange(NA_SPAN)
    q_cols = (np.arange(N_COL_BLOCKS) * NA_QBLOCK)[:, None] + np.arange(NA_QBLOCK)
    start_q = c0[q_cols]
    mask = (key_cols[:, None, :] >= start_q[:, :, None]) & (key_cols[:, None, :] < start_q[:, :, None] + NA_COLS)
    dc_idx = np.clip(key_cols[:, None, :] - q_cols[:, :, None] + NA_COLS - 1, 0, 2 * NA_COLS - 2)
    return key_cols, mask, dc_idx


def neighborhood_attention(q, k, v, rpb):
    B, T, H, hd = q.shape
    rows = T // GRID_W
    kr = min(NA_ROWS, rows)
    qg = q.reshape(B, rows, GRID_W, H, hd)
    kg = k.reshape(B, rows, GRID_W, H, hd)
    vg = v.reshape(B, rows, GRID_W, H, hd)
    key_cols, mask_np, dc_np = na_column_tables()
    col_mask = jnp.asarray(mask_np)[None, None, :, :, None, :]
    dc_idx = jnp.asarray(dc_np)
    rpb = rpb.astype(jnp.float32)
    scale = hd ** -0.5

    def row_step(r):
        r0 = jnp.clip(r - kr // 2, 0, rows - kr)
        k_rows = lax.dynamic_slice_in_dim(kg, r0, kr, axis=1)
        v_rows = lax.dynamic_slice_in_dim(vg, r0, kr, axis=1)
        k_blk = k_rows[:, :, key_cols]
        v_blk = v_rows[:, :, key_cols]
        q_row = lax.dynamic_index_in_dim(qg, r, axis=1, keepdims=False)
        q_row = q_row.reshape(B, N_COL_BLOCKS, NA_QBLOCK, H, hd)
        s = jnp.einsum('bjqhd,brjkhd->bhjqrk', q_row, k_blk) * scale
        dr_idx = r0 + jnp.arange(kr) - r + NA_ROWS - 1
        bias = rpb[:, dr_idx[:, None, None, None], dc_idx[None]]
        s = s + bias.transpose(0, 2, 3, 1, 4)[None]
        s = jnp.where(col_mask, s, -jnp.inf)
        p = jax.nn.softmax(s.reshape(B, H, N_COL_BLOCKS, NA_QBLOCK, kr * NA_SPAN), axis=-1)
        p = p.reshape(B, H, N_COL_BLOCKS, NA_QBLOCK, kr, NA_SPAN)
        o = jnp.einsum('bhjqrk,brjkhd->bjqhd', p, v_blk)
        return o.reshape(B, GRID_W, H, hd)

    out = lax.map(row_step, jnp.arange(rows))
    return out.transpose(1, 0, 2, 3, 4).reshape(B, T, H, hd)


def gla_chunked(q, k, v, g, inclusive):
    B, T, H, DK = q.shape
    DV = v.shape[-1]
    n = T // GLA_CHUNK

    def to_chunks(a):
        return a.astype(jnp.float32).reshape(B, n, GLA_CHUNK, H, a.shape[-1]).transpose(1, 0, 3, 2, 4)

    qc, kc, vc, gc = to_chunks(q), to_chunks(k), to_chunks(v), to_chunks(g)
    idx = jnp.arange(GLA_CHUNK)
    mask = (idx[:, None] >= idx[None, :]) if inclusive else (idx[:, None] > idx[None, :])

    def step(S, inp):
        qi, ki, vi, gi = inp
        b = jnp.cumsum(gi, axis=2)
        o_inter = jnp.einsum('bhcd,bhde->bhce', qi * jnp.exp(b), S)
        diff = b[:, :, :, None, :] - b[:, :, None, :, :]
        decay = jnp.exp(jnp.where(mask[:, :, None], diff, -jnp.inf))
        a = jnp.einsum('bhid,bhjd,bhijd->bhij', qi, ki, decay)
        o_intra = jnp.einsum('bhij,bhje->bhie', a, vi)
        b_last = b[:, :, -1:, :]
        S = jnp.exp(b_last[:, :, 0, :, None]) * S + jnp.einsum('bhjd,bhje->bhde', ki * jnp.exp(b_last - b), vi)
        return S, o_inter + o_intra

    S0 = jnp.zeros((B, H, DK, DV), jnp.float32)
    _, o = lax.scan(step, S0, (qc, kc, vc, gc))
    return o.transpose(1, 0, 3, 2, 4).reshape(B, T, H, DV)


def depthwise_conv(u, w, b):
    C = u.shape[-1]
    y = lax.conv_general_dilated(u, w[:, None, :].astype(u.dtype), window_strides=(1,),
                                 padding=((CONV_WIDTH // 2, CONV_WIDTH // 2),),
                                 dimension_numbers=('NWC', 'WIO', 'NWC'), feature_group_count=C)
    return y + b.astype(u.dtype)


def encoder_layer(x, norm1_w, w_in, qn_w, kn_w, rpb, w_a2_f, b_a_f, w_a2_b, b_a_b, gla_norm_w,
                  w_na_proj, w_gla_proj, w_out, norm2_w, w_up, conv_w, conv_b, w_down):
    B, T, _ = x.shape
    dt = x.dtype
    xn = rmsnorm(x, norm1_w)
    proj = xn @ w_in
    (q_na, k_na, v_na, q_g, k_g, v_g, og_g, lr_f, lr_b, gate_na, gate_gla) = jnp.split(proj, IN_SPLITS, axis=-1)

    qh = head_rmsnorm_f32(q_na.reshape(B, T, NA_HEADS, NA_HEAD_DIM), qn_w)
    kh = head_rmsnorm_f32(k_na.reshape(B, T, NA_HEADS, NA_HEAD_DIM), kn_w)
    vh = v_na.reshape(B, T, NA_HEADS, NA_HEAD_DIM).astype(jnp.float32)
    o_na = neighborhood_attention(qh, kh, vh, rpb).reshape(B, T, NA_WIDTH).astype(dt)
    y_na = o_na @ w_na_proj

    qg = q_g.reshape(B, T, GLA_HEADS, GLA_DK).astype(jnp.float32) * (GLA_DK ** -0.5)
    kg = k_g.reshape(B, T, GLA_HEADS, GLA_DK)
    vg = v_g.reshape(B, T, GLA_HEADS, GLA_DV)
    g_f = jax.nn.log_sigmoid((lr_f @ w_a2_f + b_a_f).astype(jnp.float32)) / GLA_GATE_NORM
    g_b = jax.nn.log_sigmoid((lr_b @ w_a2_b + b_a_b).astype(jnp.float32)) / GLA_GATE_NORM
    g_f = g_f.reshape(B, T, GLA_HEADS, GLA_DK)
    g_b = g_b.reshape(B, T, GLA_HEADS, GLA_DK)
    o_f = gla_chunked(qg, kg, vg, g_f, inclusive=True)
    o_b = jnp.flip(gla_chunked(jnp.flip(qg, 1), jnp.flip(kg, 1), jnp.flip(vg, 1), jnp.flip(g_b, 1),
                               inclusive=False), 1)
    o_g = head_rmsnorm_f32(o_f + o_b, gla_norm_w).reshape(B, T, GLA_VDIM)
    o_g = (o_g * jax.nn.silu(og_g.astype(jnp.float32))).astype(dt)
    y_gla = o_g @ w_gla_proj

    mix = (jax.nn.sigmoid(gate_na.astype(jnp.float32)) * y_na.astype(jnp.float32)
           + jax.nn.sigmoid(gate_gla.astype(jnp.float32)) * y_gla.astype(jnp.float32)).astype(dt)
    h = x + mix @ w_out

    hn = rmsnorm(h, norm2_w)
    u = depthwise_conv(hn @ w_up, conv_w, conv_b)
    a, b = jnp.split(u, 2, axis=-1)
    f = (jax.nn.gelu(a.astype(jnp.float32)) * b.astype(jnp.float32)).astype(dt)
    return h + f @ w_down


def setup_inputs(seed: int = 0) -> dict:
    key = jax.random.key(seed)
    ks = jax.random.split(key, 24)
    f32 = jnp.float32

    def nrm(k, shape, scale):
        return jax.random.normal(k, shape, f32) * scale

    return {
        "x_prompt": nrm(ks[0], (BATCH, SEQ, D_MODEL), 1.0),
        "x_sample": nrm(ks[1], (DEC_BATCH, DEC_SEQ, D_MODEL), 1.0),
        "norm1_w": 1.0 + nrm(ks[2], (DEPTH, D_MODEL), 0.02),
        "w_in": nrm(ks[3], (DEPTH, D_MODEL, IN_COLS), D_MODEL ** -0.5),
        "qn_w": 1.0 + nrm(ks[4], (DEPTH, NA_HEAD_DIM), 0.02),
        "kn_w": 1.0 + nrm(ks[5], (DEPTH, NA_HEAD_DIM), 0.02),
        "rpb": nrm(ks[6], (DEPTH, NA_HEADS, 2 * NA_ROWS - 1, 2 * NA_COLS - 1), 0.02),
        "w_a2_f": nrm(ks[7], (DEPTH, GLA_GATE_RANK, GLA_KDIM), GLA_GATE_RANK ** -0.5),
        "b_a_f": nrm(ks[8], (DEPTH, GLA_KDIM), 0.01),
        "w_a2_b": nrm(ks[9], (DEPTH, GLA_GATE_RANK, GLA_KDIM), GLA_GATE_RANK ** -0.5),
        "b_a_b": nrm(ks[10], (DEPTH, GLA_KDIM), 0.01),
        "gla_norm_w": 1.0 + nrm(ks[11], (DEPTH, GLA_DV), 0.02),
        "w_na_proj": nrm(ks[12], (DEPTH, NA_WIDTH, D_MODEL), NA_WIDTH ** -0.5),
        "w_gla_proj": nrm(ks[13], (DEPTH, GLA_VDIM, D_MODEL), GLA_VDIM ** -0.5),
        "w_out": nrm(ks[14], (DEPTH, D_MODEL, D_MODEL), D_MODEL ** -0.5),
        "norm2_w": 1.0 + nrm(ks[15], (DEPTH, D_MODEL), 0.02),
        "w_up": nrm(ks[16], (DEPTH, D_MODEL, 2 * D_FF), D_MODEL ** -0.5),
        "conv_w": nrm(ks[17], (DEPTH, CONV_WIDTH, 2 * D_FF), CONV_WIDTH ** -0.5),
        "conv_b": nrm(ks[18], (DEPTH, 2 * D_FF), 0.01),
        "w_down": nrm(ks[19], (DEPTH, D_FF, D_MODEL), D_FF ** -0.5),
    }


def reference(x_prompt, x_sample, norm1_w, w_in, qn_w, kn_w, rpb, w_a2_f, b_a_f, w_a2_b, b_a_b,
              gla_norm_w, w_na_proj, w_gla_proj, w_out, norm2_w, w_up, conv_w, conv_b, w_down):
    def trunk(x):
        for l in range(DEPTH):
            x = encoder_layer(x, norm1_w[l], w_in[l], qn_w[l], kn_w[l], rpb[l], w_a2_f[l], b_a_f[l],
                              w_a2_b[l], b_a_b[l], gla_norm_w[l], w_na_proj[l], w_gla_proj[l], w_out[l],
                              norm2_w[l], w_up[l], conv_w[l], conv_b[l], w_down[l])
        return x

    y_prompt = trunk(x_prompt)
    y_sample = trunk(x_sample)
    return (y_prompt, y_sample)
```

```python
import functools

import numpy as np
import jax
import jax.numpy as jnp
from jax import lax
from jax.experimental import pallas as pl
from jax.experimental.pallas import tpu as pltpu

F32 = jnp.float32
BF16 = jnp.bfloat16

D_MODEL = 1024
GRID_W = 64
NA_HEADS = 8
NA_HEAD_DIM = 64
NA_WIDTH = NA_HEADS * NA_HEAD_DIM
NA_ROWS = 8
NA_COLS = 16
GLA_HEADS = 4
GLA_DK = 128
GLA_DV = 128
GLA_WIDTH = GLA_HEADS * GLA_DK
GLA_GATE_RANK = 16
GLA_GATE_NORM = 16.0
D_FF = 2816
EPS = 1e-6

NEG = -1e30
GLA_CHUNK = 128
GLA_FAST_LIMIT = 40.0
FFN_COLS = 256
FFN_CHUNKS = D_FF // FFN_COLS
HALO = 16
V7X_VMEM_LIMIT = 56 * 1024 * 1024


def _const_spec(shape):
    zeros = (0,) * len(shape)
    return pl.BlockSpec(shape, lambda *_: zeros, pipeline_mode=pl.Buffered(1))


def _dot(a, b):
    return jnp.dot(a, b, preferred_element_type=F32)


def _dot_nt(a, b):
    return lax.dot_general(a, b, (((1,), (1,)), ((), ())), preferred_element_type=F32)


def _dot_tn(a, b):
    return lax.dot_general(a, b, (((0,), (0,)), ((), ())), preferred_element_type=F32)


def _seg_cumsum(g, row_in_seg, seg, reverse):
    n = g.shape[0]
    s = 1
    while s < seg:
        if reverse:
            shifted = pltpu.roll(g, n - s, 0)
            keep = row_in_seg < seg - s
        else:
            shifted = pltpu.roll(g, s, 0)
            keep = row_in_seg >= s
        g = g + jnp.where(keep, shifted, 0.0)
        s *= 2
    return g


def _proj_kernel(x_ref, n1_ref, wq_ref, wk_ref, wv_ref, wqg_ref, wkg_ref, wvg_ref, wog_ref,
                 wlr_ref, wgn_ref, wgg_ref, bd_ref, qnw_ref, knw_ref, w2_ref, ba_ref,
                 qn_o, kn_o, vn_o, qg_o, kg_o, vg_o, og_o, bf_o, bb_o, gn_o, gg_o, xn_s):
    x = x_ref[...]
    ms = jnp.mean(x * x, axis=-1, keepdims=True)
    xn_s[...] = (x * lax.rsqrt(ms + EPS) * n1_ref[...]).astype(BF16)

    def proj(w_ref):
        return _dot(xn_s[...], w_ref[...])

    def head_norm(y, w_ref, scale):
        y2 = y * y
        hi = y2.astype(BF16)
        lo = (y2 - hi.astype(F32)).astype(BF16)
        ssq = _dot(hi, bd_ref[...]) + _dot(lo, bd_ref[...])
        return y * lax.rsqrt(ssq * (1.0 / NA_HEAD_DIM) + EPS) * (w_ref[...] * scale)

    qn_o[...] = head_norm(proj(wq_ref), qnw_ref, NA_HEAD_DIM ** -0.5).astype(BF16)
    kn_o[...] = head_norm(proj(wk_ref), knw_ref, 1.0).astype(BF16)
    vn_o[...] = proj(wv_ref).astype(BF16)
    qg_o[...] = (proj(wqg_ref) * (GLA_DK ** -0.5)).astype(BF16)
    kg_o[...] = proj(wkg_ref).astype(BF16)
    vg_o[...] = proj(wvg_ref).astype(BF16)
    og_o[...] = proj(wog_ref).astype(BF16)
    gn_o[...] = proj(wgn_ref).astype(BF16)
    gg_o[...] = proj(wgg_ref).astype(BF16)

    lr = proj(wlr_ref)
    z = _dot(lr.astype(BF16), w2_ref[...]) + ba_ref[...]
    g = (jnp.minimum(z, 0.0) - jnp.log1p(jnp.exp(-jnp.abs(z)))) * (1.0 / GLA_GATE_NORM)
    row_in_seg = lax.broadcasted_iota(jnp.int32, (g.shape[0], GLA_WIDTH), 0) & (GLA_CHUNK - 1)
    bf_o[...] = _seg_cumsum(g[:, :GLA_WIDTH], row_in_seg, GLA_CHUNK, reverse=False)
    bb_o[...] = _seg_cumsum(g[:, GLA_WIDTH:], row_in_seg, GLA_CHUNK, reverse=True)


def _proj(x2, p, tm):
    n = x2.shape[0]
    wide = (NA_WIDTH,) * 7
    out_shapes = ([jax.ShapeDtypeStruct((n, w), BF16) for w in wide]
                  + [jax.ShapeDtypeStruct((n, GLA_WIDTH), F32)] * 2
                  + [jax.ShapeDtypeStruct((n, D_MODEL), BF16)] * 2)
    out_specs = [pl.BlockSpec((tm, s.shape[1]), lambda i: (i, 0)) for s in out_shapes]
    consts = [p["norm1_w"], p["w_q"], p["w_k"], p["w_v"], p["w_qg"], p["w_kg"], p["w_vg"], p["w_og"],
              p["w_lr"], p["w_gn"], p["w_gg"], p["bd"], p["qn_w"], p["kn_w"], p["w_a2"], p["b_a"]]
    return pl.pallas_call(
        _proj_kernel,
        grid=(n // tm,),
        in_specs=[pl.BlockSpec((tm, D_MODEL), lambda i: (i, 0))] + [_const_spec(c.shape) for c in consts],
        out_specs=out_specs,
        out_shape=out_shapes,
        scratch_shapes=[pltpu.VMEM((tm, D_MODEL), BF16)],
        compiler_params=pltpu.CompilerParams(dimension_semantics=("arbitrary",),
                                             vmem_limit_bytes=V7X_VMEM_LIMIT),
        name="proj",
    )(x2, *consts)


NA_QROWS = 8
NA_QTOK = NA_QROWS * GRID_W
NA_KROWS = 2 * NA_QROWS
NA_KTOK = NA_KROWS * GRID_W
NA_SLAB = 256


def _na_bias_table(rpb):
    qi, qc = np.arange(NA_QTOK) // GRID_W, np.arange(NA_QTOK) % GRID_W
    kr, kc = np.arange(NA_KTOK) // GRID_W, np.arange(NA_KTOK) % GRID_W
    dr = kr[None, :] - qi[:, None] + (NA_ROWS - 1 - NA_QROWS // 2)
    c0 = np.clip(qc - NA_COLS // 2, 0, GRID_W - NA_COLS)
    col_ok = (kc[None, :] >= c0[:, None]) & (kc[None, :] < c0[:, None] + NA_COLS)
    dc = np.clip(kc[None, :] - qc[:, None] + NA_COLS - 1, 0, 2 * NA_COLS - 2)
    ok = (dr >= 0) & (dr <= 2 * NA_ROWS - 2) & col_ok
    tab = rpb.astype(F32)[:, np.clip(dr, 0, 2 * NA_ROWS - 2), dc]
    return jnp.where(jnp.asarray(ok)[None], tab, NEG)


def _na_kernel(q_ref, kp_ref, kc_ref, kn_ref, vp_ref, vc_ref, vn_ref, bias_ref, o_ref,
               ku_s, vu_s, rm_s, *, rows):
    g = pl.program_id(1)
    half = NA_QTOK // 2
    ku_s[0:half] = kp_ref[0, half:NA_QTOK]
    ku_s[half:half + NA_QTOK] = kc_ref[0]
    ku_s[half + NA_QTOK:NA_KTOK] = kn_ref[0, 0:half]
    vu_s[0:half] = vp_ref[0, half:NA_QTOK]
    vu_s[half:half + NA_QTOK] = vc_ref[0]
    vu_s[half + NA_QTOK:NA_KTOK] = vn_ref[0, 0:half]

    qi = lax.broadcasted_iota(jnp.int32, (NA_QTOK, NA_KTOK), 0) >> 6
    kr = lax.broadcasted_iota(jnp.int32, (NA_QTOK, NA_KTOK), 1) >> 6
    r = NA_QROWS * g + qi
    r0 = jnp.clip(r - NA_ROWS // 2, 0, rows - NA_ROWS)
    w = r0 - NA_QROWS * g + NA_QROWS // 2
    rm_s[...] = jnp.where((kr >= w) & (kr < w + NA_ROWS), 0.0, NEG)

    lane = lax.broadcasted_iota(jnp.int32, (NA_QTOK, NA_SLAB), 1)
    for slab in range(NA_WIDTH // NA_SLAB):
        lanes = slice(slab * NA_SLAB, (slab + 1) * NA_SLAB)
        q_slab = q_ref[0, :, lanes]
        acc = jnp.zeros((NA_QTOK, NA_SLAB), F32)
        for hh in range(NA_SLAB // NA_HEAD_DIM):
            h = slab * (NA_SLAB // NA_HEAD_DIM) + hh
            in_head = (lane >= hh * NA_HEAD_DIM) & (lane < (hh + 1) * NA_HEAD_DIM)
            qm = jnp.where(in_head, q_slab, jnp.zeros_like(q_slab))
            s = _dot_nt(qm, ku_s[:, lanes]) + bias_ref[h] + rm_s[...]
            m = jnp.max(s, axis=-1, keepdims=True)
            p = jnp.exp(s - m)
            l = jnp.sum(p, axis=-1, keepdims=True)
            o = _dot(p.astype(BF16), vu_s[:, lanes]) / l
            acc = jnp.where(in_head, o, acc)
        o_ref[0, :, lanes] = acc.astype(BF16)


def _na(qn, kn, vn, bias, rows):
    b, t, _ = qn.shape
    ng = rows // NA_QROWS
    blk = (1, NA_QTOK, NA_WIDTH)
    cur = pl.BlockSpec(blk, lambda i, g: (i, g, 0))
    prev = pl.BlockSpec(blk, lambda i, g: (i, jnp.maximum(g - 1, 0), 0))
    nxt = pl.BlockSpec(blk, lambda i, g: (i, jnp.minimum(g + 1, ng - 1), 0))
    return pl.pallas_call(
        functools.partial(_na_kernel, rows=rows),
        grid=(b, ng),
        in_specs=[cur, prev, cur, nxt, prev, cur, nxt, _const_spec(bias.shape)],
        out_specs=cur,
        out_shape=jax.ShapeDtypeStruct((b, t, NA_WIDTH), BF16),
        scratch_shapes=[pltpu.VMEM((NA_KTOK, NA_WIDTH), BF16), pltpu.VMEM((NA_KTOK, NA_WIDTH), BF16),
                        pltpu.VMEM((NA_QTOK, NA_KTOK), F32)],
        compiler_params=pltpu.CompilerParams(dimension_semantics=("arbitrary", "arbitrary"),
                                             vmem_limit_bytes=V7X_VMEM_LIMIT),
        name="na",
    )(qn, kn, kn, kn, vn, vn, vn, bias)


def _gla_kernel(*refs, reverse, final, n_chunks):
    if final:
        q_ref, k_ref, v_ref, b_ref, of_ref, og_ref, nw_ref, o_ref, st_s, tmp_s = refs
    else:
        q_ref, k_ref, v_ref, b_ref, o_ref, st_s, tmp_s = refs
    c_len = GLA_CHUNK

    @pl.when(pl.program_id(1) == 0)
    def _():
        st_s[...] = jnp.zeros_like(st_s)

    ii = lax.broadcasted_iota(jnp.int32, (c_len, c_len), 0)
    jj = lax.broadcasted_iota(jnp.int32, (c_len, c_len), 1)
    pair_ok = (jj > ii) if reverse else (jj <= ii)
    row_id = lax.broadcasted_iota(jnp.int32, (c_len, 1), 0)

    def emit(rows, hl, o):
        if final:
            o = o + of_ref[0, rows, hl]
            o = o * lax.rsqrt(jnp.mean(o * o, axis=-1, keepdims=True) + EPS) * nw_ref[...]
            og = og_ref[0, rows, hl].astype(F32)
            o_ref[0, rows, hl] = (o * (og * jax.nn.sigmoid(og))).astype(o_ref.dtype)
        else:
            o_ref[0, rows, hl] = o

    def chunk_body(c, carry, *, fast):
        cc = (n_chunks - 1 - c) if reverse else c
        r0 = pl.multiple_of(cc * c_len, c_len)
        rows = pl.ds(r0, c_len)
        for h in range(GLA_HEADS):
            hl = slice(h * GLA_DK, (h + 1) * GLA_DK)
            q = q_ref[0, rows, hl].astype(F32)
            k = k_ref[0, rows, hl].astype(F32)
            v = v_ref[0, rows, hl]
            b = b_ref[0, rows, hl]
            tot = b[0:1] if reverse else b[c_len - 1:c_len]
            qf = q * jnp.exp(b)
            kt = k * jnp.exp(tot - b)
            st = st_s[h]
            o_inter = _dot_nt(qf.astype(BF16), st.astype(BF16))
            if fast:
                kf = k * jnp.exp(-b)
                a = _dot_nt(qf.astype(BF16), kf.astype(BF16))
                a = jnp.where(pair_ok, a, 0.0)
                o_intra = _dot(a.astype(BF16), v)
            else:
                tmp_s[0] = k
                tmp_s[1] = v.astype(F32)
                tmp_s[2] = b

                def key_step(j, acc):
                    kj = tmp_s[0, pl.ds(j, 1), :]
                    vj = tmp_s[1, pl.ds(j, 1), :]
                    bj = tmp_s[2, pl.ds(j, 1), :]
                    ok = (row_id < j) if reverse else (row_id >= j)
                    e = jnp.exp(jnp.where(ok, b - bj, -jnp.inf))
                    a_col = jnp.sum(q * kj * e, axis=-1, keepdims=True)
                    return acc + a_col * vj

                o_intra = lax.fori_loop(0, c_len, key_step, jnp.zeros((c_len, GLA_DV), F32))
            st_s[h] = st * jnp.exp(tot) + _dot_tn(v, kt.astype(BF16))
            emit(rows, hl, o_inter + o_intra)
        return carry

    fast_ok = jnp.min(b_ref[...]) >= -GLA_FAST_LIMIT

    @pl.when(fast_ok)
    def _():
        lax.fori_loop(0, n_chunks, functools.partial(chunk_body, fast=True), 0)

    @pl.when(jnp.logical_not(fast_ok))
    def _():
        lax.fori_loop(0, n_chunks, functools.partial(chunk_body, fast=False), 0)


def _gla(qg, kg, vg, bsum, tb, reverse, o_fwd=None, og=None, norm_w=None):
    b, t, _ = qg.shape
    nt = t // tb
    final = o_fwd is not None
    blk = (1, tb, GLA_WIDTH)
    if reverse:
        spec = pl.BlockSpec(blk, lambda i, j: (i, nt - 1 - j, 0))
    else:
        spec = pl.BlockSpec(blk, lambda i, j: (i, j, 0))
    args = [qg, kg, vg, bsum]
    in_specs = [spec] * 4
    if final:
        args += [o_fwd, og, norm_w]
        in_specs += [spec, spec, _const_spec(norm_w.shape)]
    return pl.pallas_call(
        functools.partial(_gla_kernel, reverse=reverse, final=final, n_chunks=tb // GLA_CHUNK),
        grid=(b, nt),
        in_specs=in_specs,
        out_specs=spec,
        out_shape=jax.ShapeDtypeStruct((b, t, GLA_WIDTH), BF16 if final else F32),
        scratch_shapes=[pltpu.VMEM((GLA_HEADS, GLA_DV, GLA_DK), F32), pltpu.VMEM((3, GLA_CHUNK, GLA_DK), F32)],
        compiler_params=pltpu.CompilerParams(dimension_semantics=("arbitrary", "arbitrary"),
                                             vmem_limit_bytes=V7X_VMEM_LIMIT),
        name="gla_bwd" if reverse else "gla_fwd",
    )(*args)


def _merge_kernel(ona_ref, og_ref, gn_ref, gg_ref, x_ref, wna_ref, wgla_ref, wout_ref, n2_ref, h_o, hn_o):
    y_na = _dot(ona_ref[...], wna_ref[...])
    y_gla = _dot(og_ref[...], wgla_ref[...])
    mix = (jax.nn.sigmoid(gn_ref[...].astype(F32)) * y_na
           + jax.nn.sigmoid(gg_ref[...].astype(F32)) * y_gla)
    h = x_ref[...] + _dot(mix.astype(BF16), wout_ref[...])
    h_o[...] = h
    ms = jnp.mean(h * h, axis=-1, keepdims=True)
    hn_o[...] = (h * lax.rsqrt(ms + EPS) * n2_ref[...]).astype(BF16)


def _merge(o_na, o_g, gn, gg, x2, p, tm):
    n = x2.shape[0]
    tok = lambda w: pl.BlockSpec((tm, w), lambda i: (i, 0))
    consts = [p["w_na_proj"], p["w_gla_proj"], p["w_out"], p["norm2_w"]]
    return pl.pallas_call(
        _merge_kernel,
        grid=(n // tm,),
        in_specs=[tok(NA_WIDTH), tok(GLA_WIDTH), tok(D_MODEL), tok(D_MODEL), tok(D_MODEL)]
        + [_const_spec(c.shape) for c in consts],
        out_specs=[tok(D_MODEL), tok(D_MODEL)],
        out_shape=[jax.ShapeDtypeStruct((n, D_MODEL), F32), jax.ShapeDtypeStruct((n, D_MODEL), BF16)],
        compiler_params=pltpu.CompilerParams(dimension_semantics=("arbitrary",),
                                             vmem_limit_bytes=V7X_VMEM_LIMIT),
        name="merge",
    )(o_na, o_g, gn, gg, x2, *consts)


def _gelu_tanh(x):
    return 0.5 * x * (1.0 + jnp.tanh(np.sqrt(2.0 / np.pi).astype(np.float32) * (x + 0.044715 * (x * x * x))))


def _ffn_kernel(hn_ref, hp_ref, hx_ref, h_ref, wup_ref, cw_ref, cb_ref, wdn_ref, y_o, hx_s, acc_s,
                *, tiles_per_seq):
    tm = hn_ref.shape[0]
    pos = pl.program_id(0) % tiles_per_seq
    prev_on = (pos > 0).astype(BF16)
    next_on = (pos < tiles_per_seq - 1).astype(BF16)
    hx_s[0:HALO] = hp_ref[...] * prev_on
    hx_s[HALO:HALO + tm] = hn_ref[...]
    hx_s[HALO + tm:HALO + tm + HALO] = hx_ref[...] * next_on
    acc_s[...] = jnp.zeros_like(acc_s)

    def conv(u, c):
        w = cw_ref[c]
        return (u[HALO - 1:HALO - 1 + tm] * w[0:1] + u[HALO:HALO + tm] * w[1:2]
                + u[HALO + 1:HALO + 1 + tm] * w[2:3] + cb_ref[c])

    def col_step(c, carry):
        hx = hx_s[...]
        ca = conv(_dot(hx, wup_ref[c]), c)
        cb = conv(_dot(hx, wup_ref[FFN_CHUNKS + c]), FFN_CHUNKS + c)
        f = (_gelu_tanh(ca) * cb).astype(BF16)
        acc_s[...] += _dot(f, wdn_ref[c])
        return carry

    lax.fori_loop(0, FFN_CHUNKS, col_step, 0)
    y_o[...] = h_ref[...] + acc_s[...]


def _ffn(hn, h, p, tm, t):
    n = hn.shape[0]
    tiles_per_seq = t // tm
    nh = tm // HALO
    last = n // HALO - 1
    tok = lambda dt: pl.BlockSpec((tm, D_MODEL), lambda i: (i, 0))
    prev = pl.BlockSpec((HALO, D_MODEL), lambda i: (jnp.maximum(i * nh - 1, 0), 0))
    nxt = pl.BlockSpec((HALO, D_MODEL), lambda i: (jnp.minimum((i + 1) * nh, last), 0))
    consts = [p["w_up"], p["conv_w"], p["conv_b"], p["w_down"]]
    return pl.pallas_call(
        functools.partial(_ffn_kernel, tiles_per_seq=tiles_per_seq),
        grid=(n // tm,),
        in_specs=[tok(BF16), prev, nxt, tok(F32)] + [_const_spec(c.shape) for c in consts],
        out_specs=tok(F32),
        out_shape=jax.ShapeDtypeStruct((n, D_MODEL), F32),
        scratch_shapes=[pltpu.VMEM((tm + 2 * HALO, D_MODEL), BF16), pltpu.VMEM((tm, D_MODEL), F32)],
        compiler_params=pltpu.CompilerParams(dimension_semantics=("arbitrary",),
                                             vmem_limit_bytes=V7X_VMEM_LIMIT),
        name="ffn",
    )(hn, hn, hn, h, *consts)


def _prepare_params(norm1_w, w_in, qn_w, kn_w, rpb, w_a2_f, b_a_f, w_a2_b, b_a_b, gla_norm_w,
                    w_na_proj, w_gla_proj, w_out, norm2_w, w_up, conv_w, conv_b, w_down):
    w_in = w_in.astype(BF16)
    sizes = (NA_WIDTH,) * 3 + (GLA_WIDTH,) * 4 + (GLA_GATE_RANK,) * 2 + (D_MODEL,) * 2
    offs = np.concatenate([[0], np.cumsum(sizes)])
    cols = [w_in[:, offs[i]:offs[i + 1]] for i in range(len(sizes))]
    w_lr = jnp.zeros((D_MODEL, 128), BF16).at[:, :2 * GLA_GATE_RANK].set(jnp.concatenate(cols[7:9], axis=1))
    w_a2 = jnp.zeros((128, 2 * GLA_WIDTH), F32)
    w_a2 = w_a2.at[:GLA_GATE_RANK, :GLA_WIDTH].set(w_a2_f)
    w_a2 = w_a2.at[GLA_GATE_RANK:2 * GLA_GATE_RANK, GLA_WIDTH:].set(w_a2_b)
    head = np.arange(NA_WIDTH) // NA_HEAD_DIM
    bd = jnp.asarray(head[:, None] == head[None, :], BF16)
    up = w_up.astype(BF16).reshape(D_MODEL, 2 * FFN_CHUNKS, FFN_COLS).transpose(1, 0, 2)
    cw = jnp.zeros((2 * FFN_CHUNKS, 8, FFN_COLS), F32).at[:, :3].set(
        conv_w.reshape(3, 2 * FFN_CHUNKS, FFN_COLS).transpose(1, 0, 2))
    return {
        "norm1_w": norm1_w.reshape(1, D_MODEL),
        "w_q": cols[0], "w_k": cols[1], "w_v": cols[2],
        "w_qg": cols[3], "w_kg": cols[4], "w_vg": cols[5], "w_og": cols[6],
        "w_lr": w_lr, "w_gn": cols[9], "w_gg": cols[10],
        "bd": bd,
        "qn_w": jnp.tile(qn_w, NA_HEADS).reshape(1, NA_WIDTH),
        "kn_w": jnp.tile(kn_w, NA_HEADS).reshape(1, NA_WIDTH),
        "w_a2": w_a2.astype(BF16),
        "b_a": jnp.concatenate([b_a_f, b_a_b]).reshape(1, 2 * GLA_WIDTH),
        "na_bias": _na_bias_table(rpb),
        "gla_norm_w": gla_norm_w.reshape(1, GLA_DV),
        "w_na_proj": w_na_proj.astype(BF16), "w_gla_proj": w_gla_proj.astype(BF16),
        "w_out": w_out.astype(BF16),
        "norm2_w": norm2_w.reshape(1, D_MODEL),
        "w_up": up, "conv_w": cw, "conv_b": conv_b.reshape(2 * FFN_CHUNKS, 1, FFN_COLS),
        "w_down": w_down.astype(BF16).reshape(FFN_CHUNKS, FFN_COLS, D_MODEL),
    }


def _trunk(x, p):
    b, t, _ = x.shape
    rows = t // GRID_W
    assert t % NA_QTOK == 0 and rows >= NA_ROWS, (b, t)
    tm = 512
    gla_tb = 512
    n = b * t
    x2 = x.reshape(n, D_MODEL)
    qn, kn, vn, qg, kg, vg, og, bf, bb, gn, gg = _proj(x2, p, tm)
    r3 = lambda a: a.reshape(b, t, a.shape[-1])
    o_na = _na(r3(qn), r3(kn), r3(vn), p["na_bias"], rows)
    o_f = _gla(r3(qg), r3(kg), r3(vg), r3(bf), gla_tb, reverse=False)
    o_g = _gla(r3(qg), r3(kg), r3(vg), r3(bb), gla_tb, reverse=True,
               o_fwd=o_f, og=r3(og), norm_w=p["gla_norm_w"])
    h, hn = _merge(o_na.reshape(n, NA_WIDTH), o_g.reshape(n, GLA_WIDTH), gn, gg, x2, p, tm)
    y = _ffn(hn, h, p, tm, t)
    return y.reshape(b, t, D_MODEL)


def kernel(x_prompt, x_sample, norm1_w, w_in, qn_w, kn_w, rpb, w_a2_f, b_a_f, w_a2_b, b_a_b, gla_norm_w,
           w_na_proj, w_gla_proj, w_out, norm2_w, w_up, conv_w, conv_b, w_down):
    weights = (norm1_w, w_in, qn_w, kn_w, rpb, w_a2_f, b_a_f, w_a2_b, b_a_b, gla_norm_w,
               w_na_proj, w_gla_proj, w_out, norm2_w, w_up, conv_w, conv_b, w_down)
    assert all(w.shape[0] == 1 for w in weights), "one layer"
    p = _prepare_params(*[w[0] for w in weights])
    return (_trunk(x_prompt, p), _trunk(x_sample, p))
```

```python
import functools

import numpy as np
import jax
import jax.numpy as jnp
from jax import lax
from jax.experimental import pallas as pl
from jax.experimental.pallas import tpu as pltpu

F32 = jnp.float32
BF16 = jnp.bfloat16

D_MODEL = 1024
GRID_W = 64
NA_HEADS = 8
NA_HEAD_DIM = 64
NA_WIDTH = NA_HEADS * NA_HEAD_DIM
NA_ROWS = 8
NA_COLS = 16
GLA_HEADS = 4
GLA_DK = 128
GLA_DV = 128
GLA_WIDTH = GLA_HEADS * GLA_DK
GLA_GATE_RANK = 16
GLA_GATE_NORM = 16.0
D_FF = 2816
EPS = 1e-6

NEG = -1e30
GLA_CHUNK = 128
GLA_FAST_LIMIT = 40.0
FFN_COLS = 256
FFN_CHUNKS = D_FF // FFN_COLS
assert FFN_CHUNKS % 2 == 1
HALO = 16
V7X_VMEM_LIMIT = 56 * 1024 * 1024


def _const_spec(shape):
    zeros = (0,) * len(shape)
    return pl.BlockSpec(shape, lambda *_: zeros, pipeline_mode=pl.Buffered(1))


def _dot(a, b):
    return jnp.dot(a, b, preferred_element_type=F32)


def _dot_nt(a, b):
    return lax.dot_general(a, b, (((1,), (1,)), ((), ())), preferred_element_type=F32)


def _dot_tn(a, b):
    return lax.dot_general(a, b, (((0,), (0,)), ((), ())), preferred_element_type=F32)


def _seg_cumsum(g, row_in_seg, seg, reverse):
    n = g.shape[0]
    s = 1
    while s < seg:
        if reverse:
            shifted = pltpu.roll(g, n - s, 0)
            keep = row_in_seg < seg - s
        else:
            shifted = pltpu.roll(g, s, 0)
            keep = row_in_seg >= s
        g = g + jnp.where(keep, shifted, 0.0)
        s *= 2
    return g


def _proj_kernel(x_ref, n1_ref, wq_ref, wk_ref, wv_ref, wqg_ref, wkg_ref, wvg_ref, wog_ref,
                 wlr_ref, wgn_ref, wgg_ref, bd_ref, qnw_ref, knw_ref, w2_ref, ba_ref,
                 qn_o, kn_o, vn_o, qg_o, kg_o, vg_o, og_o, bf_o, bb_o, gn_o, gg_o, xn_s):
    x = x_ref[...]
    ms = jnp.mean(x * x, axis=-1, keepdims=True)
    xn_s[...] = (x * lax.rsqrt(ms + EPS) * n1_ref[...]).astype(BF16)

    def proj(w_ref):
        return _dot(xn_s[...], w_ref[...])

    def head_norm(y, w_ref, scale):
        y2 = y * y
        hi = y2.astype(BF16)
        lo = (y2 - hi.astype(F32)).astype(BF16)
        ssq = _dot(hi, bd_ref[...]) + _dot(lo, bd_ref[...])
        return y * lax.rsqrt(ssq * (1.0 / NA_HEAD_DIM) + EPS) * (w_ref[...] * scale)

    qn_o[...] = head_norm(proj(wq_ref), qnw_ref, NA_HEAD_DIM ** -0.5).astype(BF16)
    kn_o[...] = head_norm(proj(wk_ref), knw_ref, 1.0).astype(BF16)
    vn_o[...] = proj(wv_ref).astype(BF16)
    qg_o[...] = (proj(wqg_ref) * (GLA_DK ** -0.5)).astype(BF16)
    kg_o[...] = proj(wkg_ref).astype(BF16)
    vg_o[...] = proj(wvg_ref).astype(BF16)
    og_o[...] = proj(wog_ref).astype(BF16)
    gn_o[...] = proj(wgn_ref).astype(BF16)
    gg_o[...] = proj(wgg_ref).astype(BF16)

    lr = proj(wlr_ref)
    z = _dot(lr.astype(BF16), w2_ref[...]) + ba_ref[...]
    g = (jnp.minimum(z, 0.0) - jnp.log1p(jnp.exp(-jnp.abs(z)))) * (1.0 / GLA_GATE_NORM)
    row_in_seg = lax.broadcasted_iota(jnp.int32, (g.shape[0], GLA_WIDTH), 0) & (GLA_CHUNK - 1)
    bf_o[...] = _seg_cumsum(g[:, :GLA_WIDTH], row_in_seg, GLA_CHUNK, reverse=False)
    bb_o[...] = _seg_cumsum(g[:, GLA_WIDTH:], row_in_seg, GLA_CHUNK, reverse=True)


def _proj(x2, p, tm):
    n = x2.shape[0]
    wide = (NA_WIDTH,) * 7
    out_shapes = ([jax.ShapeDtypeStruct((n, w), BF16) for w in wide]
                  + [jax.ShapeDtypeStruct((n, GLA_WIDTH), F32)] * 2
                  + [jax.ShapeDtypeStruct((n, D_MODEL), BF16)] * 2)
    out_specs = [pl.BlockSpec((tm, s.shape[1]), lambda i: (i, 0)) for s in out_shapes]
    consts = [p["norm1_w"], p["w_q"], p["w_k"], p["w_v"], p["w_qg"], p["w_kg"], p["w_vg"], p["w_og"],
              p["w_lr"], p["w_gn"], p["w_gg"], p["bd"], p["qn_w"], p["kn_w"], p["w_a2"], p["b_a"]]
    return pl.pallas_call(
        _proj_kernel,
        grid=(n // tm,),
        in_specs=[pl.BlockSpec((tm, D_MODEL), lambda i: (i, 0))] + [_const_spec(c.shape) for c in consts],
        out_specs=out_specs,
        out_shape=out_shapes,
        scratch_shapes=[pltpu.VMEM((tm, D_MODEL), BF16)],
        compiler_params=pltpu.CompilerParams(dimension_semantics=("arbitrary",),
                                             vmem_limit_bytes=V7X_VMEM_LIMIT),
        name="proj",
    )(x2, *consts)


NA_QROWS = 8
NA_QTOK = NA_QROWS * GRID_W
NA_KTOK = 2 * NA_QTOK
NA_SUB_ROWS = 4
NA_SUB_TOK = NA_SUB_ROWS * GRID_W
NA_SLAB_ROWS = NA_SUB_ROWS + NA_ROWS
NA_SLAB_TOK = NA_SLAB_ROWS * GRID_W
NA_LANES = 256
NA_GRP_HEADS = NA_LANES // NA_HEAD_DIM
NA_BLK = 32


def _na_bias_table(rpb):
    qc = np.arange(GRID_W)
    c0 = np.clip(qc - NA_COLS // 2, 0, GRID_W - NA_COLS)
    col_ok = (qc[None, :] >= c0[:, None]) & (qc[None, :] < c0[:, None] + NA_COLS)
    dc = np.clip(qc[None, :] - qc[:, None] + NA_COLS - 1, 0, 2 * NA_COLS - 2)
    sel = (dc[None] == np.arange(2 * NA_COLS - 1)[:, None, None]) & col_ok[None]
    t1 = jnp.einsum("hdc,cqk->hdqk", rpb.astype(F32), jnp.asarray(sel, F32), precision=lax.Precision.HIGHEST)
    t1 = jnp.where(jnp.asarray(col_ok)[None, None], t1, NEG)
    base = NA_ROWS - 1 - NA_ROWS // 2
    per_row = [t1[:, base - qi:base - qi + NA_SLAB_ROWS].transpose(0, 2, 1, 3) for qi in range(NA_SUB_ROWS)]
    tab = jnp.stack(per_row, axis=1)
    qi = np.arange(NA_SUB_ROWS)[:, None]
    kr = np.arange(NA_SLAB_ROWS)[None, :]
    first = [qi, np.full_like(qi, NA_ROWS // 2), np.zeros_like(qi)]
    row_ok = np.stack([(kr >= f) & (kr < f + NA_ROWS) for f in first])
    tab = jnp.where(jnp.asarray(row_ok)[:, None, :, None, :, None], tab[None], NEG)
    return tab.reshape(3, NA_HEADS, NA_SUB_TOK, NA_SLAB_TOK)


def _na_kernel(q_ref, kp_ref, kc_ref, kn_ref, vp_ref, vc_ref, vn_ref, bias_ref, o_ref,
               ku_s, vu_s, s_s, p_s):
    g = pl.program_id(1)
    last_g = pl.num_programs(1) - 1
    half = NA_QTOK // 2
    ku_s[0:half] = kp_ref[0, half:NA_QTOK]
    ku_s[half:half + NA_QTOK] = kc_ref[0]
    ku_s[half + NA_QTOK:NA_KTOK] = kn_ref[0, 0:half]
    vu_s[0:half] = vp_ref[0, half:NA_QTOK]
    vu_s[half:half + NA_QTOK] = vc_ref[0]
    vu_s[half + NA_QTOK:NA_KTOK] = vn_ref[0, 0:half]

    variant = (jnp.where(g == 0, 1, 0), jnp.where(g == last_g, 2, 0))

    lane_q = lax.broadcasted_iota(jnp.int32, (NA_SUB_TOK, NA_LANES), 1)
    unit = 0
    for j in range(NA_QROWS // NA_SUB_ROWS):
        q_rows = slice(j * NA_SUB_TOK, (j + 1) * NA_SUB_TOK)
        k_rows = slice(j * NA_SUB_TOK, j * NA_SUB_TOK + NA_SLAB_TOK)
        for grp in range(NA_WIDTH // NA_LANES):
            lanes = slice(grp * NA_LANES, (grp + 1) * NA_LANES)
            q_grp = q_ref[0, q_rows, lanes]
            for hh in range(NA_GRP_HEADS):
                h = grp * NA_GRP_HEADS + hh
                slot = unit % 2
                unit += 1
                lo, hi = hh * NA_HEAD_DIM, (hh + 1) * NA_HEAD_DIM
                qm = jnp.where((lane_q >= lo) & (lane_q < hi), q_grp, jnp.zeros_like(q_grp))
                s_s[slot] = _dot_nt(qm, ku_s[k_rows, lanes])
                inv_l = []
                for rb in range(NA_SUB_TOK // NA_BLK):
                    blk = slice(rb * NA_BLK, (rb + 1) * NA_BLK)
                    s = s_s[slot, blk, :] + bias_ref[variant[j], h, blk, :]
                    p = jnp.exp(s - jnp.max(s, axis=-1, keepdims=True))
                    inv_l.append(1.0 / jnp.sum(p, axis=-1, keepdims=True))
                    p_s[slot, blk, :] = p.astype(BF16)
                o = _dot(p_s[slot], vu_s[k_rows, lanes])
                o_ref[0, q_rows, grp * NA_LANES + lo:grp * NA_LANES + hi] = (
                    o[:, lo:hi] * jnp.concatenate(inv_l, axis=0)).astype(BF16)


def _na(qn, kn, vn, bias, rows):
    b, t, _ = qn.shape
    ng = rows // NA_QROWS
    blk = (1, NA_QTOK, NA_WIDTH)
    cur = pl.BlockSpec(blk, lambda i, g: (i, g, 0))
    prev = pl.BlockSpec(blk, lambda i, g: (i, jnp.maximum(g - 1, 0), 0))
    nxt = pl.BlockSpec(blk, lambda i, g: (i, jnp.minimum(g + 1, ng - 1), 0))
    return pl.pallas_call(
        _na_kernel,
        grid=(b, ng),
        in_specs=[cur, prev, cur, nxt, prev, cur, nxt, _const_spec(bias.shape)],
        out_specs=cur,
        out_shape=jax.ShapeDtypeStruct((b, t, NA_WIDTH), BF16),
        scratch_shapes=[pltpu.VMEM((NA_KTOK, NA_WIDTH), BF16), pltpu.VMEM((NA_KTOK, NA_WIDTH), BF16),
                        pltpu.VMEM((2, NA_SUB_TOK, NA_SLAB_TOK), F32),
                        pltpu.VMEM((2, NA_SUB_TOK, NA_SLAB_TOK), BF16)],
        compiler_params=pltpu.CompilerParams(dimension_semantics=("arbitrary", "arbitrary"),
                                             vmem_limit_bytes=V7X_VMEM_LIMIT),
        name="na",
    )(qn, kn, kn, kn, vn, vn, vn, bias)


def _gla_kernel(*refs, reverse, final, n_chunks):
    if final:
        q_ref, k_ref, v_ref, b_ref, of_ref, og_ref, nw_ref, o_ref, st_s, tmp_s = refs
    else:
        q_ref, k_ref, v_ref, b_ref, o_ref, st_s, tmp_s = refs
    c_len = GLA_CHUNK

    @pl.when(pl.program_id(1) == 0)
    def _():
        st_s[...] = jnp.zeros_like(st_s)

    ii = lax.broadcasted_iota(jnp.int32, (c_len, c_len), 0)
    jj = lax.broadcasted_iota(jnp.int32, (c_len, c_len), 1)
    pair_ok = (jj > ii) if reverse else (jj <= ii)
    row_id = lax.broadcasted_iota(jnp.int32, (c_len, 1), 0)

    def emit(rows, hl, o):
        if final:
            o = o + of_ref[0, rows, hl]
            o = o * lax.rsqrt(jnp.mean(o * o, axis=-1, keepdims=True) + EPS) * nw_ref[...]
            og = og_ref[0, rows, hl].astype(F32)
            o_ref[0, rows, hl] = (o * (og * jax.nn.sigmoid(og))).astype(o_ref.dtype)
        else:
            o_ref[0, rows, hl] = o

    def chunk_body(c, carry, *, fast):
        cc = (n_chunks - 1 - c) if reverse else c
        r0 = pl.multiple_of(cc * c_len, c_len)
        rows = pl.ds(r0, c_len)
        for h in range(GLA_HEADS):
            hl = slice(h * GLA_DK, (h + 1) * GLA_DK)
            q = q_ref[0, rows, hl].astype(F32)
            k = k_ref[0, rows, hl].astype(F32)
            v = v_ref[0, rows, hl]
            b = b_ref[0, rows, hl]
            tot = b[0:1] if reverse else b[c_len - 1:c_len]
            qf = q * jnp.exp(b)
            kt = k * jnp.exp(tot - b)
            st = st_s[h]
            o_inter = _dot_nt(qf.astype(BF16), st.astype(BF16))
            if fast:
                kf = k * jnp.exp(-b)
                a = _dot_nt(qf.astype(BF16), kf.astype(BF16))
                a = jnp.where(pair_ok, a, 0.0)
                o_intra = _dot(a.astype(BF16), v)
            else:
                tmp_s[0] = k
                tmp_s[1] = v.astype(F32)
                tmp_s[2] = b

                def key_step(j, acc):
                    kj = tmp_s[0, pl.ds(j, 1), :]
                    vj = tmp_s[1, pl.ds(j, 1), :]
                    bj = tmp_s[2, pl.ds(j, 1), :]
                    ok = (row_id < j) if reverse else (row_id >= j)
                    e = jnp.exp(jnp.where(ok, b - bj, -jnp.inf))
                    a_col = jnp.sum(q * kj * e, axis=-1, keepdims=True)
                    return acc + a_col * vj

                o_intra = lax.fori_loop(0, c_len, key_step, jnp.zeros((c_len, GLA_DV), F32))
            st_s[h] = st * jnp.exp(tot) + _dot_tn(v, kt.astype(BF16))
            emit(rows, hl, o_inter + o_intra)
        return carry

    fast_ok = jnp.min(b_ref[...]) >= -GLA_FAST_LIMIT

    @pl.when(fast_ok)
    def _():
        lax.fori_loop(0, n_chunks, functools.partial(chunk_body, fast=True), 0, unroll=True)

    @pl.when(jnp.logical_not(fast_ok))
    def _():
        lax.fori_loop(0, n_chunks, functools.partial(chunk_body, fast=False), 0)


def _gla(qg, kg, vg, bsum, tb, reverse, o_fwd=None, og=None, norm_w=None):
    b, t, _ = qg.shape
    nt = t // tb
    final = o_fwd is not None
    blk = (1, tb, GLA_WIDTH)
    if reverse:
        spec = pl.BlockSpec(blk, lambda i, j: (i, nt - 1 - j, 0))
    else:
        spec = pl.BlockSpec(blk, lambda i, j: (i, j, 0))
    args = [qg, kg, vg, bsum]
    in_specs = [spec] * 4
    if final:
        args += [o_fwd, og, norm_w]
        in_specs += [spec, spec, _const_spec(norm_w.shape)]
    return pl.pallas_call(
        functools.partial(_gla_kernel, reverse=reverse, final=final, n_chunks=tb // GLA_CHUNK),
        grid=(b, nt),
        in_specs=in_specs,
        out_specs=spec,
        out_shape=jax.ShapeDtypeStruct((b, t, GLA_WIDTH), BF16 if final else F32),
        scratch_shapes=[pltpu.VMEM((GLA_HEADS, GLA_DV, GLA_DK), F32), pltpu.VMEM((3, GLA_CHUNK, GLA_DK), F32)],
        compiler_params=pltpu.CompilerParams(dimension_semantics=("arbitrary", "arbitrary"),
                                             vmem_limit_bytes=V7X_VMEM_LIMIT),
        name="gla_bwd" if reverse else "gla_fwd",
    )(*args)


def _merge_kernel(ona_ref, og_ref, gn_ref, gg_ref, x_ref, wna_ref, wgla_ref, wout_ref, n2_ref, h_o, hn_o):
    y_na = _dot(ona_ref[...], wna_ref[...])
    y_gla = _dot(og_ref[...], wgla_ref[...])
    mix = (jax.nn.sigmoid(gn_ref[...].astype(F32)) * y_na
           + jax.nn.sigmoid(gg_ref[...].astype(F32)) * y_gla)
    h = x_ref[...] + _dot(mix.astype(BF16), wout_ref[...])
    h_o[...] = h
    ms = jnp.mean(h * h, axis=-1, keepdims=True)
    hn_o[...] = (h * lax.rsqrt(ms + EPS) * n2_ref[...]).astype(BF16)


def _merge(o_na, o_g, gn, gg, x2, p, tm):
    n = x2.shape[0]
    tok = lambda w: pl.BlockSpec((tm, w), lambda i: (i, 0))
    consts = [p["w_na_proj"], p["w_gla_proj"], p["w_out"], p["norm2_w"]]
    return pl.pallas_call(
        _merge_kernel,
        grid=(n // tm,),
        in_specs=[tok(NA_WIDTH), tok(GLA_WIDTH), tok(D_MODEL), tok(D_MODEL), tok(D_MODEL)]
        + [_const_spec(c.shape) for c in consts],
        out_specs=[tok(D_MODEL), tok(D_MODEL)],
        out_shape=[jax.ShapeDtypeStruct((n, D_MODEL), F32), jax.ShapeDtypeStruct((n, D_MODEL), BF16)],
        compiler_params=pltpu.CompilerParams(dimension_semantics=("arbitrary",),
                                             vmem_limit_bytes=V7X_VMEM_LIMIT),
        name="merge",
    )(o_na, o_g, gn, gg, x2, *consts)


def _gelu_tanh(x):
    return 0.5 * x * (1.0 + jnp.tanh(np.sqrt(2.0 / np.pi).astype(np.float32) * (x + 0.044715 * (x * x * x))))


def _ffn_kernel(hn_ref, hp_ref, hx_ref, h_ref, wup_ref, cw_ref, cb_ref, wdn_ref, y_o, hx_s, u_s, acc_s,
                *, tiles_per_seq):
    tm = hn_ref.shape[0]
    pos = pl.program_id(0) % tiles_per_seq
    prev_on = (pos > 0).astype(BF16)
    next_on = (pos < tiles_per_seq - 1).astype(BF16)
    hx_s[0:HALO] = hp_ref[...] * prev_on
    hx_s[HALO:HALO + tm] = hn_ref[...]
    hx_s[HALO + tm:HALO + tm + HALO] = hx_ref[...] * next_on
    acc_s[...] = jnp.zeros_like(acc_s)

    def up(c, slot):
        hx = hx_s[...]
        u_s[slot, 0] = _dot(hx, wup_ref[c])
        u_s[slot, 1] = _dot(hx, wup_ref[FFN_CHUNKS + c])

    def conv(slot, part, c):
        w = cw_ref[c]
        return (u_s[slot, part, HALO - 1:HALO - 1 + tm, :] * w[0:1] + u_s[slot, part, HALO:HALO + tm, :] * w[1:2]
                + u_s[slot, part, HALO + 1:HALO + 1 + tm, :] * w[2:3] + cb_ref[c])

    def down(c, slot):
        f = (_gelu_tanh(conv(slot, 0, c)) * conv(slot, 1, FFN_CHUNKS + c)).astype(BF16)
        acc_s[...] += _dot(f, wdn_ref[c])

    up(0, 0)

    def pair_step(i, carry):
        c = 2 * i
        up(c + 1, 1)
        down(c, 0)
        up(c + 2, 0)
        down(c + 1, 1)
        return carry

    lax.fori_loop(0, FFN_CHUNKS // 2, pair_step, 0)
    down(FFN_CHUNKS - 1, 0)
    y_o[...] = h_ref[...] + acc_s[...]


def _ffn(hn, h, p, tm, t):
    n = hn.shape[0]
    tiles_per_seq = t // tm
    nh = tm // HALO
    last = n // HALO - 1
    tok = lambda dt: pl.BlockSpec((tm, D_MODEL), lambda i: (i, 0))
    prev = pl.BlockSpec((HALO, D_MODEL), lambda i: (jnp.maximum(i * nh - 1, 0), 0))
    nxt = pl.BlockSpec((HALO, D_MODEL), lambda i: (jnp.minimum((i + 1) * nh, last), 0))
    consts = [p["w_up"], p["conv_w"], p["conv_b"], p["w_down"]]
    return pl.pallas_call(
        functools.partial(_ffn_kernel, tiles_per_seq=tiles_per_seq),
        grid=(n // tm,),
        in_specs=[tok(BF16), prev, nxt, tok(F32)] + [_const_spec(c.shape) for c in consts],
        out_specs=tok(F32),
        out_shape=jax.ShapeDtypeStruct((n, D_MODEL), F32),
        scratch_shapes=[pltpu.VMEM((tm + 2 * HALO, D_MODEL), BF16),
                        pltpu.VMEM((2, 2, tm + 2 * HALO, FFN_COLS), F32), pltpu.VMEM((tm, D_MODEL), F32)],
        compiler_params=pltpu.CompilerParams(dimension_semantics=("arbitrary",),
                                             vmem_limit_bytes=V7X_VMEM_LIMIT),
        name="ffn",
    )(hn, hn, hn, h, *consts)


def _prepare_params(norm1_w, w_in, qn_w, kn_w, rpb, w_a2_f, b_a_f, w_a2_b, b_a_b, gla_norm_w,
                    w_na_proj, w_gla_proj, w_out, norm2_w, w_up, conv_w, conv_b, w_down):
    w_in = w_in.astype(BF16)
    sizes = (NA_WIDTH,) * 3 + (GLA_WIDTH,) * 4 + (GLA_GATE_RANK,) * 2 + (D_MODEL,) * 2
    offs = np.concatenate([[0], np.cumsum(sizes)])
    cols = [w_in[:, offs[i]:offs[i + 1]] for i in range(len(sizes))]
    w_lr = jnp.zeros((D_MODEL, 128), BF16).at[:, :2 * GLA_GATE_RANK].set(jnp.concatenate(cols[7:9], axis=1))
    w_a2 = jnp.zeros((128, 2 * GLA_WIDTH), F32)
    w_a2 = w_a2.at[:GLA_GATE_RANK, :GLA_WIDTH].set(w_a2_f)
    w_a2 = w_a2.at[GLA_GATE_RANK:2 * GLA_GATE_RANK, GLA_WIDTH:].set(w_a2_b)
    head = np.arange(NA_WIDTH) // NA_HEAD_DIM
    bd = jnp.asarray(head[:, None] == head[None, :], BF16)
    up = w_up.astype(BF16).reshape(D_MODEL, 2 * FFN_CHUNKS, FFN_COLS).transpose(1, 0, 2)
    cw = jnp.zeros((2 * FFN_CHUNKS, 8, FFN_COLS), F32).at[:, :3].set(
        conv_w.reshape(3, 2 * FFN_CHUNKS, FFN_COLS).transpose(1, 0, 2))
    return {
        "norm1_w": norm1_w.reshape(1, D_MODEL),
        "w_q": cols[0], "w_k": cols[1], "w_v": cols[2],
        "w_qg": cols[3], "w_kg": cols[4], "w_vg": cols[5], "w_og": cols[6],
        "w_lr": w_lr, "w_gn": cols[9], "w_gg": cols[10],
        "bd": bd,
        "qn_w": jnp.tile(qn_w, NA_HEADS).reshape(1, NA_WIDTH),
        "kn_w": jnp.tile(kn_w, NA_HEADS).reshape(1, NA_WIDTH),
        "w_a2": w_a2.astype(BF16),
        "b_a": jnp.concatenate([b_a_f, b_a_b]).reshape(1, 2 * GLA_WIDTH),
        "na_bias": _na_bias_table(rpb),
        "gla_norm_w": gla_norm_w.reshape(1, GLA_DV),
        "w_na_proj": w_na_proj.astype(BF16), "w_gla_proj": w_gla_proj.astype(BF16),
        "w_out": w_out.astype(BF16),
        "norm2_w": norm2_w.reshape(1, D_MODEL),
        "w_up": up, "conv_w": cw, "conv_b": conv_b.reshape(2 * FFN_CHUNKS, 1, FFN_COLS),
        "w_down": w_down.astype(BF16).reshape(FFN_CHUNKS, FFN_COLS, D_MODEL),
    }


def _trunk(x, p):
    b, t, _ = x.shape
    rows = t // GRID_W
    assert t % NA_QTOK == 0 and rows >= NA_ROWS, (b, t)
    tm = 512
    gla_tb = 512
    n = b * t
    x2 = x.reshape(n, D_MODEL)
    qn, kn, vn, qg, kg, vg, og, bf, bb, gn, gg = _proj(x2, p, tm)
    r3 = lambda a: a.reshape(b, t, a.shape[-1])
    o_na = _na(r3(qn), r3(kn), r3(vn), p["na_bias"], rows)
    o_f = _gla(r3(qg), r3(kg), r3(vg), r3(bf), gla_tb, reverse=False)
    o_g = _gla(r3(qg), r3(kg), r3(vg), r3(bb), gla_tb, reverse=True,
               o_fwd=o_f, og=r3(og), norm_w=p["gla_norm_w"])
    h, hn = _merge(o_na.reshape(n, NA_WIDTH), o_g.reshape(n, GLA_WIDTH), gn, gg, x2, p, tm)
    y = _ffn(hn, h, p, tm, t)
    return y.reshape(b, t, D_MODEL)


def kernel(x_prompt, x_sample, norm1_w, w_in, qn_w, kn_w, rpb, w_a2_f, b_a_f, w_a2_b, b_a_b, gla_norm_w,
           w_na_proj, w_gla_proj, w_out, norm2_w, w_up, conv_w, conv_b, w_down):
    weights = (norm1_w, w_in, qn_w, kn_w, rpb, w_a2_f, b_a_f, w_a2_b, b_a_b, gla_norm_w,
               w_na_proj, w_gla_proj, w_out, norm2_w, w_up, conv_w, conv_b, w_down)
    assert all(w.shape[0] == 1 for w in weights), "one layer"
    p = _prepare_params(*[w[0] for w in weights])
    return (_trunk(x_prompt, p), _trunk(x_sample, p))
```

```python
import functools

import numpy as np
import jax
import jax.numpy as jnp
from jax import lax
from jax.experimental import pallas as pl
from jax.experimental.pallas import tpu as pltpu

F32 = jnp.float32
BF16 = jnp.bfloat16

D_MODEL = 1024
GRID_W = 64
NA_HEADS = 8
NA_HEAD_DIM = 64
NA_WIDTH = NA_HEADS * NA_HEAD_DIM
NA_ROWS = 8
NA_COLS = 16
GLA_HEADS = 4
GLA_DK = 128
GLA_DV = 128
GLA_WIDTH = GLA_HEADS * GLA_DK
GLA_GATE_RANK = 16
GLA_GATE_NORM = 16.0
D_FF = 2816
EPS = 1e-6

NEG = -1e30
GLA_CHUNK = 128
GLA_FAST_LIMIT = 40.0
FFN_COLS = 256
FFN_CHUNKS = D_FF // FFN_COLS
assert FFN_CHUNKS % 2 == 1 and FFN_CHUNKS >= 3
HALO = 16
V7X_VMEM_LIMIT = 56 * 1024 * 1024


def _const_spec(shape):
    zeros = (0,) * len(shape)
    return pl.BlockSpec(shape, lambda *_: zeros, pipeline_mode=pl.Buffered(1))


def _dot(a, b):
    return jnp.dot(a, b, preferred_element_type=F32)


def _dot_nt(a, b):
    return lax.dot_general(a, b, (((1,), (1,)), ((), ())), preferred_element_type=F32)


def _dot_tn(a, b):
    return lax.dot_general(a, b, (((0,), (0,)), ((), ())), preferred_element_type=F32)


def _seg_cumsum(g, row_in_seg, seg, reverse):
    n = g.shape[0]
    s = 1
    while s < seg:
        if reverse:
            shifted = pltpu.roll(g, n - s, 0)
            keep = row_in_seg < seg - s
        else:
            shifted = pltpu.roll(g, s, 0)
            keep = row_in_seg >= s
        g = g + jnp.where(keep, shifted, 0.0)
        s *= 2
    return g


def _proj_kernel(x_ref, n1_ref, wq_ref, wk_ref, wv_ref, wqg_ref, wkg_ref, wvg_ref, wog_ref,
                 wlr_ref, wgn_ref, wgg_ref, bd_ref, qnw_ref, knw_ref, w2_ref, ba_ref,
                 qn_o, kn_o, vn_o, qg_o, kg_o, vg_o, og_o, bf_o, bb_o, gn_o, gg_o, xn_s):
    x = x_ref[...]
    ms = jnp.mean(x * x, axis=-1, keepdims=True)
    xn_s[...] = (x * lax.rsqrt(ms + EPS) * n1_ref[...]).astype(BF16)

    def proj(w_ref):
        return _dot(xn_s[...], w_ref[...])

    def head_norm(y, w_ref, scale):
        y2 = y * y
        hi = y2.astype(BF16)
        lo = (y2 - hi.astype(F32)).astype(BF16)
        ssq = _dot(hi, bd_ref[...]) + _dot(lo, bd_ref[...])
        return y * lax.rsqrt(ssq * (1.0 / NA_HEAD_DIM) + EPS) * (w_ref[...] * scale)

    lr = proj(wlr_ref)
    vn_o[...] = proj(wv_ref).astype(BF16)

    z = _dot(lr.astype(BF16), w2_ref[...]) + ba_ref[...]
    g = (jnp.minimum(z, 0.0) - jnp.log1p(jnp.exp(-jnp.abs(z)))) * (1.0 / GLA_GATE_NORM)
    row_in_seg = lax.broadcasted_iota(jnp.int32, (g.shape[0], GLA_WIDTH), 0) & (GLA_CHUNK - 1)
    bf_o[...] = _seg_cumsum(g[:, :GLA_WIDTH], row_in_seg, GLA_CHUNK, reverse=False)
    bb_o[...] = _seg_cumsum(g[:, GLA_WIDTH:], row_in_seg, GLA_CHUNK, reverse=True)

    q = proj(wq_ref)
    kg_o[...] = proj(wkg_ref).astype(BF16)
    qn_o[...] = head_norm(q, qnw_ref, NA_HEAD_DIM ** -0.5).astype(BF16)
    k = proj(wk_ref)
    vg_o[...] = proj(wvg_ref).astype(BF16)
    kn_o[...] = head_norm(k, knw_ref, 1.0).astype(BF16)
    qg_o[...] = (proj(wqg_ref) * (GLA_DK ** -0.5)).astype(BF16)
    og_o[...] = proj(wog_ref).astype(BF16)
    gn_o[...] = proj(wgn_ref).astype(BF16)
    gg_o[...] = proj(wgg_ref).astype(BF16)


def _proj(x2, p, tm):
    n = x2.shape[0]
    wide = (NA_WIDTH,) * 7
    out_shapes = ([jax.ShapeDtypeStruct((n, w), BF16) for w in wide]
                  + [jax.ShapeDtypeStruct((n, GLA_WIDTH), F32)] * 2
                  + [jax.ShapeDtypeStruct((n, D_MODEL), BF16)] * 2)
    out_specs = [pl.BlockSpec((tm, s.shape[1]), lambda i: (i, 0)) for s in out_shapes]
    consts = [p["norm1_w"], p["w_q"], p["w_k"], p["w_v"], p["w_qg"], p["w_kg"], p["w_vg"], p["w_og"],
              p["w_lr"], p["w_gn"], p["w_gg"], p["bd"], p["qn_w"], p["kn_w"], p["w_a2"], p["b_a"]]
    return pl.pallas_call(
        _proj_kernel,
        grid=(n // tm,),
        in_specs=[pl.BlockSpec((tm, D_MODEL), lambda i: (i, 0))] + [_const_spec(c.shape) for c in consts],
        out_specs=out_specs,
        out_shape=out_shapes,
        scratch_shapes=[pltpu.VMEM((tm, D_MODEL), BF16)],
        compiler_params=pltpu.CompilerParams(dimension_semantics=("arbitrary",),
                                             vmem_limit_bytes=V7X_VMEM_LIMIT),
        name="proj",
    )(x2, *consts)


NA_QROWS = 8
NA_QTOK = NA_QROWS * GRID_W
NA_KTOK = 2 * NA_QTOK
NA_SUB_ROWS = 4
NA_SUB_TOK = NA_SUB_ROWS * GRID_W
NA_SLAB_ROWS = NA_SUB_ROWS + NA_ROWS
NA_SLAB_TOK = NA_SLAB_ROWS * GRID_W
NA_LANES = 256
NA_GRP_HEADS = NA_LANES // NA_HEAD_DIM
NA_BLK = 32


def _na_bias_table(rpb):
    qc = np.arange(GRID_W)
    c0 = np.clip(qc - NA_COLS // 2, 0, GRID_W - NA_COLS)
    col_ok = (qc[None, :] >= c0[:, None]) & (qc[None, :] < c0[:, None] + NA_COLS)
    dc = np.clip(qc[None, :] - qc[:, None] + NA_COLS - 1, 0, 2 * NA_COLS - 2)
    sel = (dc[None] == np.arange(2 * NA_COLS - 1)[:, None, None]) & col_ok[None]
    t1 = jnp.einsum("hdc,cqk->hdqk", rpb.astype(F32), jnp.asarray(sel, F32), precision=lax.Precision.HIGHEST)
    t1 = jnp.where(jnp.asarray(col_ok)[None, None], t1, NEG)
    base = NA_ROWS - 1 - NA_ROWS // 2
    per_row = [t1[:, base - qi:base - qi + NA_SLAB_ROWS].transpose(0, 2, 1, 3) for qi in range(NA_SUB_ROWS)]
    tab = jnp.stack(per_row, axis=1)
    qi = np.arange(NA_SUB_ROWS)[:, None]
    kr = np.arange(NA_SLAB_ROWS)[None, :]
    first = [qi, np.full_like(qi, NA_ROWS // 2), np.zeros_like(qi)]
    row_ok = np.stack([(kr >= f) & (kr < f + NA_ROWS) for f in first])
    tab = jnp.where(jnp.asarray(row_ok)[:, None, :, None, :, None], tab[None], NEG)
    return tab.reshape(3, NA_HEADS, NA_SUB_TOK, NA_SLAB_TOK)


def _na_kernel(q_ref, kp_ref, kc_ref, kn_ref, vp_ref, vc_ref, vn_ref, bias_ref, o_ref,
               ku_s, vu_s, s_s, p_s):
    g = pl.program_id(1)
    last_g = pl.num_programs(1) - 1
    half = NA_QTOK // 2
    ku_s[0:half] = kp_ref[0, half:NA_QTOK]
    ku_s[half:half + NA_QTOK] = kc_ref[0]
    ku_s[half + NA_QTOK:NA_KTOK] = kn_ref[0, 0:half]
    vu_s[0:half] = vp_ref[0, half:NA_QTOK]
    vu_s[half:half + NA_QTOK] = vc_ref[0]
    vu_s[half + NA_QTOK:NA_KTOK] = vn_ref[0, 0:half]

    variant = (jnp.where(g == 0, 1, 0), jnp.where(g == last_g, 2, 0))

    lane_q = lax.broadcasted_iota(jnp.int32, (NA_SUB_TOK, NA_LANES), 1)
    unit = 0
    for j in range(NA_QROWS // NA_SUB_ROWS):
        q_rows = slice(j * NA_SUB_TOK, (j + 1) * NA_SUB_TOK)
        k_rows = slice(j * NA_SUB_TOK, j * NA_SUB_TOK + NA_SLAB_TOK)
        for grp in range(NA_WIDTH // NA_LANES):
            lanes = slice(grp * NA_LANES, (grp + 1) * NA_LANES)
            q_grp = q_ref[0, q_rows, lanes]
            for hh in range(NA_GRP_HEADS):
                h = grp * NA_GRP_HEADS + hh
                slot = unit % 2
                unit += 1
                lo, hi = hh * NA_HEAD_DIM, (hh + 1) * NA_HEAD_DIM
                qm = jnp.where((lane_q >= lo) & (lane_q < hi), q_grp, jnp.zeros_like(q_grp))
                s_s[slot] = _dot_nt(qm, ku_s[k_rows, lanes])
                inv_l = []
                for rb in range(NA_SUB_TOK // NA_BLK):
                    blk = slice(rb * NA_BLK, (rb + 1) * NA_BLK)
                    s = s_s[slot, blk, :] + bias_ref[variant[j], h, blk, :]
                    p = jnp.exp(s - jnp.max(s, axis=-1, keepdims=True))
                    inv_l.append(1.0 / jnp.sum(p, axis=-1, keepdims=True))
                    p_s[slot, blk, :] = p.astype(BF16)
                o = _dot(p_s[slot], vu_s[k_rows, lanes])
                o_ref[0, q_rows, grp * NA_LANES + lo:grp * NA_LANES + hi] = (
                    o[:, lo:hi] * jnp.concatenate(inv_l, axis=0)).astype(BF16)


def _na(qn, kn, vn, bias, rows):
    b, t, _ = qn.shape
    ng = rows // NA_QROWS
    blk = (1, NA_QTOK, NA_WIDTH)
    cur = pl.BlockSpec(blk, lambda i, g: (i, g, 0))
    prev = pl.BlockSpec(blk, lambda i, g: (i, jnp.maximum(g - 1, 0), 0))
    nxt = pl.BlockSpec(blk, lambda i, g: (i, jnp.minimum(g + 1, ng - 1), 0))
    return pl.pallas_call(
        _na_kernel,
        grid=(b, ng),
        in_specs=[cur, prev, cur, nxt, prev, cur, nxt, _const_spec(bias.shape)],
        out_specs=cur,
        out_shape=jax.ShapeDtypeStruct((b, t, NA_WIDTH), BF16),
        scratch_shapes=[pltpu.VMEM((NA_KTOK, NA_WIDTH), BF16), pltpu.VMEM((NA_KTOK, NA_WIDTH), BF16),
                        pltpu.VMEM((2, NA_SUB_TOK, NA_SLAB_TOK), F32),
                        pltpu.VMEM((2, NA_SUB_TOK, NA_SLAB_TOK), BF16)],
        compiler_params=pltpu.CompilerParams(dimension_semantics=("arbitrary", "arbitrary"),
                                             vmem_limit_bytes=V7X_VMEM_LIMIT),
        name="na",
    )(qn, kn, kn, kn, vn, vn, vn, bias)


def _gla_kernel(*refs, reverse, final, n_chunks, n_seq):
    if final:
        q_ref, k_ref, v_ref, b_ref, of_ref, og_ref, nw_ref, o_ref, st_s, tmp_s = refs
    else:
        q_ref, k_ref, v_ref, b_ref, o_ref, st_s, tmp_s = refs
    c_len = GLA_CHUNK

    @pl.when(pl.program_id(1) == 0)
    def _():
        st_s[...] = jnp.zeros_like(st_s)

    ii = lax.broadcasted_iota(jnp.int32, (c_len, c_len), 0)
    jj = lax.broadcasted_iota(jnp.int32, (c_len, c_len), 1)
    pair_ok = (jj > ii) if reverse else (jj <= ii)
    row_id = lax.broadcasted_iota(jnp.int32, (c_len, 1), 0)

    def emit(s, rows, hl, o):
        if final:
            o = o + of_ref[s, rows, hl]
            o = o * lax.rsqrt(jnp.mean(o * o, axis=-1, keepdims=True) + EPS) * nw_ref[...]
            og = og_ref[s, rows, hl].astype(F32)
            o_ref[s, rows, hl] = (o * (og * jax.nn.sigmoid(og))).astype(o_ref.dtype)
        else:
            o_ref[s, rows, hl] = o

    def chunk_body(c, carry, *, fast):
        cc = (n_chunks - 1 - c) if reverse else c
        r0 = pl.multiple_of(cc * c_len, c_len)
        rows = pl.ds(r0, c_len)
        for s, h in [(s, h) for s in range(n_seq) for h in range(GLA_HEADS)]:
            hl = slice(h * GLA_DK, (h + 1) * GLA_DK)
            q = q_ref[s, rows, hl].astype(F32)
            k = k_ref[s, rows, hl].astype(F32)
            v = v_ref[s, rows, hl]
            b = b_ref[s, rows, hl]
            tot = b[0:1] if reverse else b[c_len - 1:c_len]
            qf = q * jnp.exp(b)
            kt = k * jnp.exp(tot - b)
            st = st_s[s, h]
            o_inter = _dot_nt(qf.astype(BF16), st.astype(BF16))
            if fast:
                kf = k * jnp.exp(-b)
                a = _dot_nt(qf.astype(BF16), kf.astype(BF16))
                a = jnp.where(pair_ok, a, 0.0)
                o_intra = _dot(a.astype(BF16), v)
            else:
                tmp_s[0] = k
                tmp_s[1] = v.astype(F32)
                tmp_s[2] = b

                def key_step(j, acc):
                    kj = tmp_s[0, pl.ds(j, 1), :]
                    vj = tmp_s[1, pl.ds(j, 1), :]
                    bj = tmp_s[2, pl.ds(j, 1), :]
                    ok = (row_id < j) if reverse else (row_id >= j)
                    e = jnp.exp(jnp.where(ok, b - bj, -jnp.inf))
                    a_col = jnp.sum(q * kj * e, axis=-1, keepdims=True)
                    return acc + a_col * vj

                o_intra = lax.fori_loop(0, c_len, key_step, jnp.zeros((c_len, GLA_DV), F32))
            st_s[s, h] = st * jnp.exp(tot) + _dot_tn(v, kt.astype(BF16))
            emit(s, rows, hl, o_inter + o_intra)
        return carry

    fast_ok = jnp.min(b_ref[...]) >= -GLA_FAST_LIMIT

    @pl.when(fast_ok)
    def _():
        lax.fori_loop(0, n_chunks, functools.partial(chunk_body, fast=True), 0, unroll=True)

    @pl.when(jnp.logical_not(fast_ok))
    def _():
        lax.fori_loop(0, n_chunks, functools.partial(chunk_body, fast=False), 0)


def _gla(qg, kg, vg, bsum, tb, reverse, o_fwd=None, og=None, norm_w=None):
    b, t, _ = qg.shape
    nt = t // tb
    final = o_fwd is not None
    n_seq = 2 if b % 2 == 0 else 1
    blk = (n_seq, tb, GLA_WIDTH)
    if reverse:
        spec = pl.BlockSpec(blk, lambda i, j: (i, nt - 1 - j, 0))
    else:
        spec = pl.BlockSpec(blk, lambda i, j: (i, j, 0))
    args = [qg, kg, vg, bsum]
    in_specs = [spec] * 4
    if final:
        args += [o_fwd, og, norm_w]
        in_specs += [spec, spec, _const_spec(norm_w.shape)]
    return pl.pallas_call(
        functools.partial(_gla_kernel, reverse=reverse, final=final, n_chunks=tb // GLA_CHUNK, n_seq=n_seq),
        grid=(b // n_seq, nt),
        in_specs=in_specs,
        out_specs=spec,
        out_shape=jax.ShapeDtypeStruct((b, t, GLA_WIDTH), BF16 if final else F32),
        scratch_shapes=[pltpu.VMEM((n_seq, GLA_HEADS, GLA_DV, GLA_DK), F32),
                        pltpu.VMEM((3, GLA_CHUNK, GLA_DK), F32)],
        compiler_params=pltpu.CompilerParams(dimension_semantics=("arbitrary", "arbitrary"),
                                             vmem_limit_bytes=V7X_VMEM_LIMIT),
        name="gla_bwd" if reverse else "gla_fwd",
    )(*args)


def _merge_kernel(ona_ref, og_ref, gn_ref, gg_ref, x_ref, wna_ref, wgla_ref, wout_ref, n2_ref, h_o, hn_o):
    y_na = _dot(ona_ref[...], wna_ref[...])
    y_gla = _dot(og_ref[...], wgla_ref[...])
    mix = (jax.nn.sigmoid(gn_ref[...].astype(F32)) * y_na
           + jax.nn.sigmoid(gg_ref[...].astype(F32)) * y_gla)
    h = x_ref[...] + _dot(mix.astype(BF16), wout_ref[...])
    h_o[...] = h
    ms = jnp.mean(h * h, axis=-1, keepdims=True)
    hn_o[...] = (h * lax.rsqrt(ms + EPS) * n2_ref[...]).astype(BF16)


def _merge(o_na, o_g, gn, gg, x2, p, tm):
    n = x2.shape[0]
    tok = lambda w: pl.BlockSpec((tm, w), lambda i: (i, 0))
    consts = [p["w_na_proj"], p["w_gla_proj"], p["w_out"], p["norm2_w"]]
    return pl.pallas_call(
        _merge_kernel,
        grid=(n // tm,),
        in_specs=[tok(NA_WIDTH), tok(GLA_WIDTH), tok(D_MODEL), tok(D_MODEL), tok(D_MODEL)]
        + [_const_spec(c.shape) for c in consts],
        out_specs=[tok(D_MODEL), tok(D_MODEL)],
        out_shape=[jax.ShapeDtypeStruct((n, D_MODEL), F32), jax.ShapeDtypeStruct((n, D_MODEL), BF16)],
        compiler_params=pltpu.CompilerParams(dimension_semantics=("arbitrary",),
                                             vmem_limit_bytes=V7X_VMEM_LIMIT),
        name="merge",
    )(o_na, o_g, gn, gg, x2, *consts)


def _gelu_tanh(x):
    return 0.5 * x * (1.0 + jnp.tanh(np.sqrt(2.0 / np.pi).astype(np.float32) * (x + 0.044715 * (x * x * x))))


def _ffn_kernel(hn_ref, hp_ref, hx_ref, h_ref, wup_ref, cw_ref, cb_ref, wdn_ref, y_o, hx_s,
                ua0_s, ub0_s, ua1_s, ub1_s, *, tiles_per_seq):
    u_s = ((ua0_s, ub0_s), (ua1_s, ub1_s))
    tm = hn_ref.shape[0]
    pos = pl.program_id(0) % tiles_per_seq
    prev_on = (pos > 0).astype(BF16)
    next_on = (pos < tiles_per_seq - 1).astype(BF16)
    hx_s[0:HALO] = hp_ref[...] * prev_on
    hx_s[HALO:HALO + tm] = hn_ref[...]
    hx_s[HALO + tm:HALO + tm + HALO] = hx_ref[...] * next_on
    y_o[...] = h_ref[...]

    def up(c, slot):
        hx = hx_s[...]
        for part in range(2):
            u_s[slot][part][...] = _dot(hx, wup_ref[part * FFN_CHUNKS + c])

    def conv(u_ref, c):
        w = cw_ref[c]
        return (u_ref[HALO - 1:HALO - 1 + tm, :] * w[0:1] + u_ref[HALO:HALO + tm, :] * w[1:2]
                + u_ref[HALO + 1:HALO + 1 + tm, :] * w[2:3] + cb_ref[c])

    def down(c, slot):
        f = (_gelu_tanh(conv(u_s[slot][0], c)) * conv(u_s[slot][1], FFN_CHUNKS + c)).astype(BF16)
        y_o[...] += _dot(f, wdn_ref[c])

    up(0, 0)

    def pair_step(i, carry):
        c = 2 * i
        up(c + 1, 1)
        down(c, 0)
        up(c + 2, 0)
        down(c + 1, 1)
        return carry

    lax.fori_loop(0, FFN_CHUNKS // 2, pair_step, 0)
    down(FFN_CHUNKS - 1, 0)


def _ffn(hn, h, p, tm, t):
    n = hn.shape[0]
    tiles_per_seq = t // tm
    nh = tm // HALO
    last = n // HALO - 1
    tok = lambda dt: pl.BlockSpec((tm, D_MODEL), lambda i: (i, 0))
    prev = pl.BlockSpec((HALO, D_MODEL), lambda i: (jnp.maximum(i * nh - 1, 0), 0))
    nxt = pl.BlockSpec((HALO, D_MODEL), lambda i: (jnp.minimum((i + 1) * nh, last), 0))
    consts = [p["w_up"], p["conv_w"], p["conv_b"], p["w_down"]]
    return pl.pallas_call(
        functools.partial(_ffn_kernel, tiles_per_seq=tiles_per_seq),
        grid=(n // tm,),
        in_specs=[tok(BF16), prev, nxt, tok(F32)] + [_const_spec(c.shape) for c in consts],
        out_specs=tok(F32),
        out_shape=jax.ShapeDtypeStruct((n, D_MODEL), F32),
        scratch_shapes=[pltpu.VMEM((tm + 2 * HALO, D_MODEL), BF16)]
        + [pltpu.VMEM((tm + 2 * HALO, FFN_COLS), F32)] * 4,
        compiler_params=pltpu.CompilerParams(dimension_semantics=("arbitrary",),
                                             vmem_limit_bytes=V7X_VMEM_LIMIT),
        name="ffn",
    )(hn, hn, hn, h, *consts)


def _prepare_params(norm1_w, w_in, qn_w, kn_w, rpb, w_a2_f, b_a_f, w_a2_b, b_a_b, gla_norm_w,
                    w_na_proj, w_gla_proj, w_out, norm2_w, w_up, conv_w, conv_b, w_down):
    w_in = w_in.astype(BF16)
    sizes = (NA_WIDTH,) * 3 + (GLA_WIDTH,) * 4 + (GLA_GATE_RANK,) * 2 + (D_MODEL,) * 2
    offs = np.concatenate([[0], np.cumsum(sizes)])
    cols = [w_in[:, offs[i]:offs[i + 1]] for i in range(len(sizes))]
    w_lr = jnp.zeros((D_MODEL, 128), BF16).at[:, :2 * GLA_GATE_RANK].set(jnp.concatenate(cols[7:9], axis=1))
    w_a2 = jnp.zeros((128, 2 * GLA_WIDTH), F32)
    w_a2 = w_a2.at[:GLA_GATE_RANK, :GLA_WIDTH].set(w_a2_f)
    w_a2 = w_a2.at[GLA_GATE_RANK:2 * GLA_GATE_RANK, GLA_WIDTH:].set(w_a2_b)
    head = np.arange(NA_WIDTH) // NA_HEAD_DIM
    bd = jnp.asarray(head[:, None] == head[None, :], BF16)
    up = w_up.astype(BF16).reshape(D_MODEL, 2 * FFN_CHUNKS, FFN_COLS).transpose(1, 0, 2)
    cw = jnp.zeros((2 * FFN_CHUNKS, 8, FFN_COLS), F32).at[:, :3].set(
        conv_w.reshape(3, 2 * FFN_CHUNKS, FFN_COLS).transpose(1, 0, 2))
    return {
        "norm1_w": norm1_w.reshape(1, D_MODEL),
        "w_q": cols[0], "w_k": cols[1], "w_v": cols[2],
        "w_qg": cols[3], "w_kg": cols[4], "w_vg": cols[5], "w_og": cols[6],
        "w_lr": w_lr, "w_gn": cols[9], "w_gg": cols[10],
        "bd": bd,
        "qn_w": jnp.tile(qn_w, NA_HEADS).reshape(1, NA_WIDTH),
        "kn_w": jnp.tile(kn_w, NA_HEADS).reshape(1, NA_WIDTH),
        "w_a2": w_a2.astype(BF16),
        "b_a": jnp.concatenate([b_a_f, b_a_b]).reshape(1, 2 * GLA_WIDTH),
        "na_bias": _na_bias_table(rpb),
        "gla_norm_w": gla_norm_w.reshape(1, GLA_DV),
        "w_na_proj": w_na_proj.astype(BF16), "w_gla_proj": w_gla_proj.astype(BF16),
        "w_out": w_out.astype(BF16),
        "norm2_w": norm2_w.reshape(1, D_MODEL),
        "w_up": up, "conv_w": cw, "conv_b": conv_b.reshape(2 * FFN_CHUNKS, 1, FFN_COLS),
        "w_down": w_down.astype(BF16).reshape(FFN_CHUNKS, FFN_COLS, D_MODEL),
    }


def _trunk(x, p):
    b, t, _ = x.shape
    rows = t // GRID_W
    assert t % NA_QTOK == 0 and rows >= NA_ROWS, (b, t)
    tm = 512
    gla_tb = 512
    n = b * t
    x2 = x.reshape(n, D_MODEL)
    qn, kn, vn, qg, kg, vg, og, bf, bb, gn, gg = _proj(x2, p, tm)
    r3 = lambda a: a.reshape(b, t, a.shape[-1])
    o_na = _na(r3(qn), r3(kn), r3(vn), p["na_bias"], rows)
    o_f = _gla(r3(qg), r3(kg), r3(vg), r3(bf), gla_tb, reverse=False)
    o_g = _gla(r3(qg), r3(kg), r3(vg), r3(bb), gla_tb, reverse=True,
               o_fwd=o_f, og=r3(og), norm_w=p["gla_norm_w"])
    h, hn = _merge(o_na.reshape(n, NA_WIDTH), o_g.reshape(n, GLA_WIDTH), gn, gg, x2, p, tm)
    y = _ffn(hn, h, p, tm, t)
    return y.reshape(b, t, D_MODEL)


def kernel(x_prompt, x_sample, norm1_w, w_in, qn_w, kn_w, rpb, w_a2_f, b_a_f, w_a2_b, b_a_b, gla_norm_w,
           w_na_proj, w_gla_proj, w_out, norm2_w, w_up, conv_w, conv_b, w_down):
    weights = (norm1_w, w_in, qn_w, kn_w, rpb, w_a2_f, b_a_f, w_a2_b, b_a_b, gla_norm_w,
               w_na_proj, w_gla_proj, w_out, norm2_w, w_up, conv_w, conv_b, w_down)
    assert all(w.shape[0] == 1 for w in weights), "one layer"
    p = _prepare_params(*[w[0] for w in weights])
    return (_trunk(x_prompt, p), _trunk(x_sample, p))
```

```python
import functools

import numpy as np
import jax
import jax.numpy as jnp
from jax import lax
from jax.experimental import pallas as pl
from jax.experimental.pallas import tpu as pltpu

F32 = jnp.float32
BF16 = jnp.bfloat16

D_MODEL = 1024
GRID_W = 64
NA_HEADS = 8
NA_HEAD_DIM = 64
NA_WIDTH = NA_HEADS * NA_HEAD_DIM
NA_ROWS = 8
NA_COLS = 16
GLA_HEADS = 4
GLA_DK = 128
GLA_DV = 128
GLA_WIDTH = GLA_HEADS * GLA_DK
GLA_GATE_RANK = 16
GLA_GATE_NORM = 16.0
D_FF = 2816
EPS = 1e-6

NEG = -1e30
GLA_CHUNK = 128
GLA_FAST_LIMIT = 40.0
FFN_COLS = 256
FFN_CHUNKS = D_FF // FFN_COLS
assert FFN_CHUNKS % 2 == 1 and FFN_CHUNKS >= 3
HALO = 16
V7X_VMEM_LIMIT = 56 * 1024 * 1024


def _const_spec(shape):
    zeros = (0,) * len(shape)
    return pl.BlockSpec(shape, lambda *_: zeros, pipeline_mode=pl.Buffered(1))


def _dot(a, b):
    return jnp.dot(a, b, preferred_element_type=F32)


def _dot_nt(a, b):
    return lax.dot_general(a, b, (((1,), (1,)), ((), ())), preferred_element_type=F32)


def _dot_tn(a, b):
    return lax.dot_general(a, b, (((0,), (0,)), ((), ())), preferred_element_type=F32)


def _seg_cumsum(g, row_in_seg, seg, reverse):
    n = g.shape[0]
    s = 1
    while s < seg:
        if reverse:
            shifted = pltpu.roll(g, n - s, 0)
            keep = row_in_seg < seg - s
        else:
            shifted = pltpu.roll(g, s, 0)
            keep = row_in_seg >= s
        g = g + jnp.where(keep, shifted, 0.0)
        s *= 2
    return g


def _proj_kernel(x_ref, n1_ref, wq_ref, wk_ref, wv_ref, wqg_ref, wkg_ref, wvg_ref, wog_ref,
                 wlr_ref, wgn_ref, wgg_ref, bd_ref, qnw_ref, knw_ref, w2_ref, ba_ref,
                 qn_o, kn_o, vn_o, qg_o, kg_o, vg_o, og_o, bf_o, bb_o, gn_o, gg_o, xn_s):
    x = x_ref[...]
    ms = jnp.mean(x * x, axis=-1, keepdims=True)
    xn_s[...] = (x * lax.rsqrt(ms + EPS) * n1_ref[...]).astype(BF16)

    def proj(w_ref):
        return _dot(xn_s[...], w_ref[...])

    def head_norm(y, w_ref, scale):
        y2 = y * y
        hi = y2.astype(BF16)
        lo = (y2 - hi.astype(F32)).astype(BF16)
        ssq = _dot(hi, bd_ref[...]) + _dot(lo, bd_ref[...])
        return y * lax.rsqrt(ssq * (1.0 / NA_HEAD_DIM) + EPS) * (w_ref[...] * scale)

    lr = proj(wlr_ref)
    vn_o[...] = proj(wv_ref).astype(BF16)

    z = _dot(lr.astype(BF16), w2_ref[...]) + ba_ref[...]
    g = (jnp.minimum(z, 0.0) - jnp.log1p(jnp.exp(-jnp.abs(z)))) * (1.0 / GLA_GATE_NORM)
    row_in_seg = lax.broadcasted_iota(jnp.int32, (g.shape[0], GLA_WIDTH), 0) & (GLA_CHUNK - 1)
    bf_o[...] = _seg_cumsum(g[:, :GLA_WIDTH], row_in_seg, GLA_CHUNK, reverse=False)
    bb_o[...] = _seg_cumsum(g[:, GLA_WIDTH:], row_in_seg, GLA_CHUNK, reverse=True)

    q = proj(wq_ref)
    kg_o[...] = proj(wkg_ref).astype(BF16)
    qn_o[...] = head_norm(q, qnw_ref, NA_HEAD_DIM ** -0.5).astype(BF16)
    k = proj(wk_ref)
    vg_o[...] = proj(wvg_ref).astype(BF16)
    kn_o[...] = head_norm(k, knw_ref, 1.0).astype(BF16)
    qg_o[...] = (proj(wqg_ref) * (GLA_DK ** -0.5)).astype(BF16)
    og_o[...] = proj(wog_ref).astype(BF16)
    gn_o[...] = proj(wgn_ref).astype(BF16)
    gg_o[...] = proj(wgg_ref).astype(BF16)


def _proj(x2, p, tm):
    n = x2.shape[0]
    wide = (NA_WIDTH,) * 7
    out_shapes = ([jax.ShapeDtypeStruct((n, w), BF16) for w in wide]
                  + [jax.ShapeDtypeStruct((n, GLA_WIDTH), F32)] * 2
                  + [jax.ShapeDtypeStruct((n, D_MODEL), BF16)] * 2)
    out_specs = [pl.BlockSpec((tm, s.shape[1]), lambda i: (i, 0)) for s in out_shapes]
    consts = [p["norm1_w"], p["w_q"], p["w_k"], p["w_v"], p["w_qg"], p["w_kg"], p["w_vg"], p["w_og"],
              p["w_lr"], p["w_gn"], p["w_gg"], p["bd"], p["qn_w"], p["kn_w"], p["w_a2"], p["b_a"]]
    return pl.pallas_call(
        _proj_kernel,
        grid=(n // tm,),
        in_specs=[pl.BlockSpec((tm, D_MODEL), lambda i: (i, 0))] + [_const_spec(c.shape) for c in consts],
        out_specs=out_specs,
        out_shape=out_shapes,
        scratch_shapes=[pltpu.VMEM((tm, D_MODEL), BF16)],
        compiler_params=pltpu.CompilerParams(dimension_semantics=("arbitrary",),
                                             vmem_limit_bytes=V7X_VMEM_LIMIT),
        name="proj",
    )(x2, *consts)


NA_QROWS_OPTIONS = (8, 16)
NA_HALO_TOK = (NA_ROWS // 2) * GRID_W
NA_SUB_ROWS = 4
NA_SUB_TOK = NA_SUB_ROWS * GRID_W
NA_SLAB_ROWS = NA_SUB_ROWS + NA_ROWS
NA_SLAB_TOK = NA_SLAB_ROWS * GRID_W
NA_LANES = 256
NA_GRP_HEADS = NA_LANES // NA_HEAD_DIM
NA_BLK = 32


def _na_bias_table(rpb):
    qc = np.arange(GRID_W)
    c0 = np.clip(qc - NA_COLS // 2, 0, GRID_W - NA_COLS)
    col_ok = (qc[None, :] >= c0[:, None]) & (qc[None, :] < c0[:, None] + NA_COLS)
    dc = np.clip(qc[None, :] - qc[:, None] + NA_COLS - 1, 0, 2 * NA_COLS - 2)
    sel = (dc[None] == np.arange(2 * NA_COLS - 1)[:, None, None]) & col_ok[None]
    t1 = jnp.einsum("hdc,cqk->hdqk", rpb.astype(F32), jnp.asarray(sel, F32), precision=lax.Precision.HIGHEST)
    t1 = jnp.where(jnp.asarray(col_ok)[None, None], t1, NEG)
    base = NA_ROWS - 1 - NA_ROWS // 2
    per_row = [t1[:, base - qi:base - qi + NA_SLAB_ROWS].transpose(0, 2, 1, 3) for qi in range(NA_SUB_ROWS)]
    tab = jnp.stack(per_row, axis=1)
    qi = np.arange(NA_SUB_ROWS)[:, None]
    kr = np.arange(NA_SLAB_ROWS)[None, :]
    first = [qi, np.full_like(qi, NA_ROWS // 2), np.zeros_like(qi)]
    row_ok = np.stack([(kr >= f) & (kr < f + NA_ROWS) for f in first])
    tab = jnp.where(jnp.asarray(row_ok)[:, None, :, None, :, None], tab[None], NEG)
    return tab.reshape(3, NA_HEADS, NA_SUB_TOK, NA_SLAB_TOK)


def _na_kernel(q_ref, kp_ref, kc_ref, kn_ref, vp_ref, vc_ref, vn_ref, bias_ref, o_ref,
               ku_s, vu_s, s_s, p_s, *, qrows):
    g = pl.program_id(1)
    last_g = pl.num_programs(1) - 1
    qtok = qrows * GRID_W
    n_sub = qrows // NA_SUB_ROWS
    ku_s[0:NA_HALO_TOK] = kp_ref[0, qtok - NA_HALO_TOK:qtok]
    ku_s[NA_HALO_TOK:NA_HALO_TOK + qtok] = kc_ref[0]
    ku_s[NA_HALO_TOK + qtok:qtok + 2 * NA_HALO_TOK] = kn_ref[0, 0:NA_HALO_TOK]
    vu_s[0:NA_HALO_TOK] = vp_ref[0, qtok - NA_HALO_TOK:qtok]
    vu_s[NA_HALO_TOK:NA_HALO_TOK + qtok] = vc_ref[0]
    vu_s[NA_HALO_TOK + qtok:qtok + 2 * NA_HALO_TOK] = vn_ref[0, 0:NA_HALO_TOK]

    variant = [0] * n_sub
    variant[0] = jnp.where(g == 0, 1, 0)
    variant[n_sub - 1] = jnp.where(g == last_g, 2, 0)

    lane_q = lax.broadcasted_iota(jnp.int32, (NA_SUB_TOK, NA_LANES), 1)
    unit = 0
    for j in range(n_sub):
        q_rows = slice(j * NA_SUB_TOK, (j + 1) * NA_SUB_TOK)
        k_rows = slice(j * NA_SUB_TOK, j * NA_SUB_TOK + NA_SLAB_TOK)
        for grp in range(NA_WIDTH // NA_LANES):
            lanes = slice(grp * NA_LANES, (grp + 1) * NA_LANES)
            q_grp = q_ref[0, q_rows, lanes]
            for hh in range(NA_GRP_HEADS):
                h = grp * NA_GRP_HEADS + hh
                slot = unit % 2
                unit += 1
                lo, hi = hh * NA_HEAD_DIM, (hh + 1) * NA_HEAD_DIM
                qm = jnp.where((lane_q >= lo) & (lane_q < hi), q_grp, jnp.zeros_like(q_grp))
                s_s[slot] = _dot_nt(qm, ku_s[k_rows, lanes])
                inv_l = []
                for rb in range(NA_SUB_TOK // NA_BLK):
                    blk = slice(rb * NA_BLK, (rb + 1) * NA_BLK)
                    s = s_s[slot, blk, :] + bias_ref[variant[j], h, blk, :]
                    p = jnp.exp(s - jnp.max(s, axis=-1, keepdims=True))
                    inv_l.append(1.0 / jnp.sum(p, axis=-1, keepdims=True))
                    p_s[slot, blk, :] = p.astype(BF16)
                o = _dot(p_s[slot], vu_s[k_rows, lanes])
                o_ref[0, q_rows, grp * NA_LANES + lo:grp * NA_LANES + hi] = (
                    o[:, lo:hi] * jnp.concatenate(inv_l, axis=0)).astype(BF16)


def _na(qn, kn, vn, bias, rows):
    b, t, _ = qn.shape
    qrows = max(q for q in NA_QROWS_OPTIONS if rows % q == 0)
    qtok = qrows * GRID_W
    ng = rows // qrows
    blk = (1, qtok, NA_WIDTH)
    cur = pl.BlockSpec(blk, lambda i, g: (i, g, 0))
    prev = pl.BlockSpec(blk, lambda i, g: (i, jnp.maximum(g - 1, 0), 0))
    nxt = pl.BlockSpec(blk, lambda i, g: (i, jnp.minimum(g + 1, ng - 1), 0))
    return pl.pallas_call(
        functools.partial(_na_kernel, qrows=qrows),
        grid=(b, ng),
        in_specs=[cur, prev, cur, nxt, prev, cur, nxt, _const_spec(bias.shape)],
        out_specs=cur,
        out_shape=jax.ShapeDtypeStruct((b, t, NA_WIDTH), BF16),
        scratch_shapes=[pltpu.VMEM((qtok + 2 * NA_HALO_TOK, NA_WIDTH), BF16),
                        pltpu.VMEM((qtok + 2 * NA_HALO_TOK, NA_WIDTH), BF16),
                        pltpu.VMEM((2, NA_SUB_TOK, NA_SLAB_TOK), F32),
                        pltpu.VMEM((2, NA_SUB_TOK, NA_SLAB_TOK), BF16)],
        compiler_params=pltpu.CompilerParams(dimension_semantics=("arbitrary", "arbitrary"),
                                             vmem_limit_bytes=V7X_VMEM_LIMIT),
        name="na",
    )(qn, kn, kn, kn, vn, vn, vn, bias)


def _gla_kernel(*refs, reverse, final, n_chunks, n_seq):
    if final:
        q_ref, k_ref, v_ref, b_ref, of_ref, og_ref, nw_ref, o_ref, st_s, tmp_s = refs
    else:
        q_ref, k_ref, v_ref, b_ref, o_ref, st_s, tmp_s = refs
    c_len = GLA_CHUNK

    @pl.when(pl.program_id(1) == 0)
    def _():
        st_s[...] = jnp.zeros_like(st_s)

    ii = lax.broadcasted_iota(jnp.int32, (c_len, c_len), 0)
    jj = lax.broadcasted_iota(jnp.int32, (c_len, c_len), 1)
    pair_ok = (jj > ii) if reverse else (jj <= ii)
    row_id = lax.broadcasted_iota(jnp.int32, (c_len, 1), 0)

    def emit(s, rows, hl, o):
        if final:
            o = o + of_ref[s, rows, hl]
            o = o * lax.rsqrt(jnp.mean(o * o, axis=-1, keepdims=True) + EPS) * nw_ref[...]
            og = og_ref[s, rows, hl].astype(F32)
            o_ref[s, rows, hl] = (o * (og * jax.nn.sigmoid(og))).astype(o_ref.dtype)
        else:
            o_ref[s, rows, hl] = o

    def chunk_body(c, carry, *, fast):
        cc = (n_chunks - 1 - c) if reverse else c
        r0 = pl.multiple_of(cc * c_len, c_len)
        rows = pl.ds(r0, c_len)
        for s, h in [(s, h) for s in range(n_seq) for h in range(GLA_HEADS)]:
            hl = slice(h * GLA_DK, (h + 1) * GLA_DK)
            q = q_ref[s, rows, hl].astype(F32)
            k = k_ref[s, rows, hl].astype(F32)
            v = v_ref[s, rows, hl]
            b = b_ref[s, rows, hl]
            tot = b[0:1] if reverse else b[c_len - 1:c_len]
            qf = q * jnp.exp(b)
            kt = k * jnp.exp(tot - b)
            st = st_s[s, h]
            o_inter = _dot_nt(qf.astype(BF16), st.astype(BF16))
            if fast:
                kf = k * jnp.exp(-b)
                a = _dot_nt(qf.astype(BF16), kf.astype(BF16))
                a = jnp.where(pair_ok, a, 0.0)
                o_intra = _dot(a.astype(BF16), v)
            else:
                tmp_s[0] = k
                tmp_s[1] = v.astype(F32)
                tmp_s[2] = b

                def key_step(j, acc):
                    kj = tmp_s[0, pl.ds(j, 1), :]
                    vj = tmp_s[1, pl.ds(j, 1), :]
                    bj = tmp_s[2, pl.ds(j, 1), :]
                    ok = (row_id < j) if reverse else (row_id >= j)
                    e = jnp.exp(jnp.where(ok, b - bj, -jnp.inf))
                    a_col = jnp.sum(q * kj * e, axis=-1, keepdims=True)
                    return acc + a_col * vj

                o_intra = lax.fori_loop(0, c_len, key_step, jnp.zeros((c_len, GLA_DV), F32))
            st_s[s, h] = st * jnp.exp(tot) + _dot_tn(v, kt.astype(BF16))
            emit(s, rows, hl, o_inter + o_intra)
        return carry

    fast_ok = jnp.min(b_ref[...]) >= -GLA_FAST_LIMIT

    @pl.when(fast_ok)
    def _():
        lax.fori_loop(0, n_chunks, functools.partial(chunk_body, fast=True), 0, unroll=True)

    @pl.when(jnp.logical_not(fast_ok))
    def _():
        lax.fori_loop(0, n_chunks, functools.partial(chunk_body, fast=False), 0)


def _gla(qg, kg, vg, bsum, tb, reverse, o_fwd=None, og=None, norm_w=None):
    b, t, _ = qg.shape
    nt = t // tb
    final = o_fwd is not None
    n_seq = 2 if b % 2 == 0 else 1
    blk = (n_seq, tb, GLA_WIDTH)
    if reverse:
        spec = pl.BlockSpec(blk, lambda i, j: (i, nt - 1 - j, 0))
    else:
        spec = pl.BlockSpec(blk, lambda i, j: (i, j, 0))
    args = [qg, kg, vg, bsum]
    in_specs = [spec] * 4
    if final:
        args += [o_fwd, og, norm_w]
        in_specs += [spec, spec, _const_spec(norm_w.shape)]
    return pl.pallas_call(
        functools.partial(_gla_kernel, reverse=reverse, final=final, n_chunks=tb // GLA_CHUNK, n_seq=n_seq),
        grid=(b // n_seq, nt),
        in_specs=in_specs,
        out_specs=spec,
        out_shape=jax.ShapeDtypeStruct((b, t, GLA_WIDTH), BF16 if final else F32),
        scratch_shapes=[pltpu.VMEM((n_seq, GLA_HEADS, GLA_DV, GLA_DK), F32),
                        pltpu.VMEM((3, GLA_CHUNK, GLA_DK), F32)],
        compiler_params=pltpu.CompilerParams(dimension_semantics=("arbitrary", "arbitrary"),
                                             vmem_limit_bytes=V7X_VMEM_LIMIT),
        name="gla_bwd" if reverse else "gla_fwd",
    )(*args)


def _merge_tile(ona_ref, og_ref, gn_ref, gg_ref, x_ref, wna_ref, wgla_ref, wout_ref, n2_ref):
    y_na = _dot(ona_ref[...], wna_ref[...])
    y_gla = _dot(og_ref[...], wgla_ref[...])
    mix = (jax.nn.sigmoid(gn_ref[...].astype(F32)) * y_na
           + jax.nn.sigmoid(gg_ref[...].astype(F32)) * y_gla)
    h = x_ref[...] + _dot(mix.astype(BF16), wout_ref[...])
    ms = jnp.mean(h * h, axis=-1, keepdims=True)
    return h, (h * lax.rsqrt(ms + EPS) * n2_ref[...]).astype(BF16)


def _gelu_tanh(x):
    return 0.5 * x * (1.0 + jnp.tanh(np.sqrt(2.0 / np.pi).astype(np.float32) * (x + 0.044715 * (x * x * x))))


def _ffn_tile(hx_s, y_o, wup_ref, cw_ref, cb_ref, wdn_ref, u_s):
    tm = y_o.shape[0]

    def up(c, slot):
        hx = hx_s[...]
        for part in range(2):
            u_s[slot][part][...] = _dot(hx, wup_ref[part * FFN_CHUNKS + c])

    def conv(u_ref, c):
        w = cw_ref[c]
        return (u_ref[HALO - 1:HALO - 1 + tm, :] * w[0:1] + u_ref[HALO:HALO + tm, :] * w[1:2]
                + u_ref[HALO + 1:HALO + 1 + tm, :] * w[2:3] + cb_ref[c])

    def gate(c, slot):
        return (_gelu_tanh(conv(u_s[slot][0], c)) * conv(u_s[slot][1], FFN_CHUNKS + c)).astype(BF16)

    def down(c, f):
        y_o[...] += _dot(f, wdn_ref[c])

    up(0, 0)

    def pair_step(i, carry):
        c = 2 * i
        f = gate(c, 0)
        up(c + 1, 1)
        down(c, f)
        f = gate(c + 1, 1)
        up(c + 2, 0)
        down(c + 1, f)
        return carry

    lax.fori_loop(0, FFN_CHUNKS // 2, pair_step, 0)
    down(FFN_CHUNKS - 1, gate(FFN_CHUNKS - 1, 0))


def _mix_ffn_kernel(ona_ref, og_ref, gn_ref, gg_ref, x_ref, wna_ref, wgla_ref, wout_ref, n2_ref,
                    wup_ref, cw_ref, cb_ref, wdn_ref, y_o, h_s, hn_s, hx_s, ua0_s, ub0_s, ua1_s, ub1_s,
                    *, tiles_per_seq):
    s = pl.program_id(0)
    n_tiles = pl.num_programs(0) - 1
    tm = y_o.shape[0]

    @pl.when(s == 0)
    def _():
        hn_s[...] = jnp.zeros_like(hn_s)
        y_o[...] = jnp.zeros_like(y_o)

    @pl.when(s < n_tiles)
    def _():
        h, hn = _merge_tile(ona_ref, og_ref, gn_ref, gg_ref, x_ref, wna_ref, wgla_ref, wout_ref, n2_ref)
        h_s[s % 2] = h
        hn_s[s % 3] = hn

    @pl.when(s > 0)
    def _():
        t = s - 1
        pos = t % tiles_per_seq
        zero = jnp.zeros((HALO, D_MODEL), BF16)
        hx_s[0:HALO] = jnp.where(pos > 0, hn_s[(t + 2) % 3, tm - HALO:tm, :], zero)
        hx_s[HALO:HALO + tm] = hn_s[t % 3]
        hx_s[HALO + tm:HALO + tm + HALO] = jnp.where(pos < tiles_per_seq - 1, hn_s[s % 3, 0:HALO, :], zero)
        y_o[...] = h_s[t % 2]
        _ffn_tile(hx_s, y_o, wup_ref, cw_ref, cb_ref, wdn_ref, ((ua0_s, ub0_s), (ua1_s, ub1_s)))


def _mix_ffn(o_na, o_g, gn, gg, x2, p, tm, t):
    n = x2.shape[0]
    n_tiles = n // tm
    tok = lambda w: pl.BlockSpec((tm, w), lambda s: (jnp.minimum(s, n_tiles - 1), 0))
    consts = [p["w_na_proj"], p["w_gla_proj"], p["w_out"], p["norm2_w"],
              p["w_up"], p["conv_w"], p["conv_b"], p["w_down"]]
    return pl.pallas_call(
        functools.partial(_mix_ffn_kernel, tiles_per_seq=t // tm),
        grid=(n_tiles + 1,),
        in_specs=[tok(NA_WIDTH), tok(GLA_WIDTH), tok(D_MODEL), tok(D_MODEL), tok(D_MODEL)]
        + [_const_spec(c.shape) for c in consts],
        out_specs=pl.BlockSpec((tm, D_MODEL), lambda s: (jnp.maximum(s - 1, 0), 0)),
        out_shape=jax.ShapeDtypeStruct((n, D_MODEL), F32),
        scratch_shapes=[pltpu.VMEM((2, tm, D_MODEL), F32), pltpu.VMEM((3, tm, D_MODEL), BF16),
                        pltpu.VMEM((tm + 2 * HALO, D_MODEL), BF16)]
        + [pltpu.VMEM((tm + 2 * HALO, FFN_COLS), F32)] * 4,
        compiler_params=pltpu.CompilerParams(dimension_semantics=("arbitrary",),
                                             vmem_limit_bytes=V7X_VMEM_LIMIT),
        name="mix_ffn",
    )(o_na, o_g, gn, gg, x2, *consts)


def _prepare_params(norm1_w, w_in, qn_w, kn_w, rpb, w_a2_f, b_a_f, w_a2_b, b_a_b, gla_norm_w,
                    w_na_proj, w_gla_proj, w_out, norm2_w, w_up, conv_w, conv_b, w_down):
    w_in = w_in.astype(BF16)
    sizes = (NA_WIDTH,) * 3 + (GLA_WIDTH,) * 4 + (GLA_GATE_RANK,) * 2 + (D_MODEL,) * 2
    offs = np.concatenate([[0], np.cumsum(sizes)])
    cols = [w_in[:, offs[i]:offs[i + 1]] for i in range(len(sizes))]
    w_lr = jnp.zeros((D_MODEL, 128), BF16).at[:, :2 * GLA_GATE_RANK].set(jnp.concatenate(cols[7:9], axis=1))
    w_a2 = jnp.zeros((128, 2 * GLA_WIDTH), F32)
    w_a2 = w_a2.at[:GLA_GATE_RANK, :GLA_WIDTH].set(w_a2_f)
    w_a2 = w_a2.at[GLA_GATE_RANK:2 * GLA_GATE_RANK, GLA_WIDTH:].set(w_a2_b)
    head = np.arange(NA_WIDTH) // NA_HEAD_DIM
    bd = jnp.asarray(head[:, None] == head[None, :], BF16)
    up = w_up.astype(BF16).reshape(D_MODEL, 2 * FFN_CHUNKS, FFN_COLS).transpose(1, 0, 2)
    cw = jnp.zeros((2 * FFN_CHUNKS, 8, FFN_COLS), F32).at[:, :3].set(
        conv_w.reshape(3, 2 * FFN_CHUNKS, FFN_COLS).transpose(1, 0, 2))
    return {
        "norm1_w": norm1_w.reshape(1, D_MODEL),
        "w_q": cols[0], "w_k": cols[1], "w_v": cols[2],
        "w_qg": cols[3], "w_kg": cols[4], "w_vg": cols[5], "w_og": cols[6],
        "w_lr": w_lr, "w_gn": cols[9], "w_gg": cols[10],
        "bd": bd,
        "qn_w": jnp.tile(qn_w, NA_HEADS).reshape(1, NA_WIDTH),
        "kn_w": jnp.tile(kn_w, NA_HEADS).reshape(1, NA_WIDTH),
        "w_a2": w_a2.astype(BF16),
        "b_a": jnp.concatenate([b_a_f, b_a_b]).reshape(1, 2 * GLA_WIDTH),
        "na_bias": _na_bias_table(rpb),
        "gla_norm_w": gla_norm_w.reshape(1, GLA_DV),
        "w_na_proj": w_na_proj.astype(BF16), "w_gla_proj": w_gla_proj.astype(BF16),
        "w_out": w_out.astype(BF16),
        "norm2_w": norm2_w.reshape(1, D_MODEL),
        "w_up": up, "conv_w": cw, "conv_b": conv_b.reshape(2 * FFN_CHUNKS, 1, FFN_COLS),
        "w_down": w_down.astype(BF16).reshape(FFN_CHUNKS, FFN_COLS, D_MODEL),
    }


def _trunk(x, p):
    b, t, _ = x.shape
    rows = t // GRID_W
    assert rows % min(NA_QROWS_OPTIONS) == 0 and rows >= NA_ROWS, (b, t)
    tm = 512
    gla_tb = 512
    n = b * t
    x2 = x.reshape(n, D_MODEL)
    qn, kn, vn, qg, kg, vg, og, bf, bb, gn, gg = _proj(x2, p, tm)
    r3 = lambda a: a.reshape(b, t, a.shape[-1])
    o_na = _na(r3(qn), r3(kn), r3(vn), p["na_bias"], rows)
    o_f = _gla(r3(qg), r3(kg), r3(vg), r3(bf), gla_tb, reverse=False)
    o_g = _gla(r3(qg), r3(kg), r3(vg), r3(bb), gla_tb, reverse=True,
               o_fwd=o_f, og=r3(og), norm_w=p["gla_norm_w"])
    y = _mix_ffn(o_na.reshape(n, NA_WIDTH), o_g.reshape(n, GLA_WIDTH), gn, gg, x2, p, tm, t)
    return y.reshape(b, t, D_MODEL)


def kernel(x_prompt, x_sample, norm1_w, w_in, qn_w, kn_w, rpb, w_a2_f, b_a_f, w_a2_b, b_a_b, gla_norm_w,
           w_na_proj, w_gla_proj, w_out, norm2_w, w_up, conv_w, conv_b, w_down):
    weights = (norm1_w, w_in, qn_w, kn_w, rpb, w_a2_f, b_a_f, w_a2_b, b_a_b, gla_norm_w,
               w_na_proj, w_gla_proj, w_out, norm2_w, w_up, conv_w, conv_b, w_down)
    assert all(w.shape[0] == 1 for w in weights), "one layer"
    p = _prepare_params(*[w[0] for w in weights])
    return (_trunk(x_prompt, p), _trunk(x_sample, p))
```

```python
import functools

import numpy as np
import jax
import jax.numpy as jnp
from jax import lax
from jax.experimental import pallas as pl
from jax.experimental.pallas import tpu as pltpu

F32 = jnp.float32
BF16 = jnp.bfloat16

D_MODEL = 1024
GRID_W = 64
NA_HEADS = 8
NA_HEAD_DIM = 64
NA_WIDTH = NA_HEADS * NA_HEAD_DIM
NA_ROWS = 8
NA_COLS = 16
GLA_HEADS = 4
GLA_DK = 128
GLA_DV = 128
GLA_WIDTH = GLA_HEADS * GLA_DK
GLA_GATE_RANK = 16
GLA_GATE_NORM = 16.0
D_FF = 2816
EPS = 1e-6

NEG = -1e30
GLA_CHUNK = 128
GLA_FAST_LIMIT = 40.0
FFN_COLS = 256
FFN_CHUNKS = D_FF // FFN_COLS
assert FFN_CHUNKS % 2 == 1 and FFN_CHUNKS >= 3
HALO = 16
V7X_VMEM_LIMIT = 56 * 1024 * 1024


def _const_spec(shape):
    zeros = (0,) * len(shape)
    return pl.BlockSpec(shape, lambda *_: zeros, pipeline_mode=pl.Buffered(1))


def _dot(a, b):
    return jnp.dot(a, b, preferred_element_type=F32)


def _dot_nt(a, b):
    return lax.dot_general(a, b, (((1,), (1,)), ((), ())), preferred_element_type=F32)


def _dot_tn(a, b):
    return lax.dot_general(a, b, (((0,), (0,)), ((), ())), preferred_element_type=F32)


def _seg_cumsum(g, row_in_seg, seg, reverse):
    n = g.shape[0]
    s = 1
    while s < seg:
        if reverse:
            shifted = pltpu.roll(g, n - s, 0)
            keep = row_in_seg < seg - s
        else:
            shifted = pltpu.roll(g, s, 0)
            keep = row_in_seg >= s
        g = g + jnp.where(keep, shifted, 0.0)
        s *= 2
    return g


def _proj_kernel(x_ref, n1_ref, wq_ref, wk_ref, wv_ref, wqg_ref, wkg_ref, wvg_ref, wog_ref,
                 wlr_ref, wgn_ref, wgg_ref, bd_ref, qnw_ref, knw_ref, w2_ref, ba_ref,
                 qn_o, kn_o, vn_o, qg_o, kg_o, vg_o, og_o, bf_o, bb_o, gn_o, gg_o, xn_s):
    x = x_ref[...]
    ms = jnp.mean(x * x, axis=-1, keepdims=True)
    xn_s[...] = (x * lax.rsqrt(ms + EPS) * n1_ref[...]).astype(BF16)

    def proj(w_ref):
        return _dot(xn_s[...], w_ref[...])

    def head_norm(y, w_ref, scale):
        y2 = y * y
        hi = y2.astype(BF16)
        lo = (y2 - hi.astype(F32)).astype(BF16)
        ssq = _dot(hi, bd_ref[...]) + _dot(lo, bd_ref[...])
        return y * lax.rsqrt(ssq * (1.0 / NA_HEAD_DIM) + EPS) * (w_ref[...] * scale)

    lr = proj(wlr_ref)
    vn_o[...] = proj(wv_ref).astype(BF16)

    z = _dot(lr.astype(BF16), w2_ref[...]) + ba_ref[...]
    g = (jnp.minimum(z, 0.0) - jnp.log1p(jnp.exp(-jnp.abs(z)))) * (1.0 / GLA_GATE_NORM)
    row_in_seg = lax.broadcasted_iota(jnp.int32, (g.shape[0], GLA_WIDTH), 0) & (GLA_CHUNK - 1)
    bf_o[...] = _seg_cumsum(g[:, :GLA_WIDTH], row_in_seg, GLA_CHUNK, reverse=False)
    bb_o[...] = _seg_cumsum(g[:, GLA_WIDTH:], row_in_seg, GLA_CHUNK, reverse=True)

    q = proj(wq_ref)
    kg_o[...] = proj(wkg_ref).astype(BF16)
    qn_o[...] = head_norm(q, qnw_ref, NA_HEAD_DIM ** -0.5).astype(BF16)
    k = proj(wk_ref)
    vg_o[...] = proj(wvg_ref).astype(BF16)
    kn_o[...] = head_norm(k, knw_ref, 1.0).astype(BF16)
    qg_o[...] = (proj(wqg_ref) * (GLA_DK ** -0.5)).astype(BF16)
    og_o[...] = proj(wog_ref).astype(BF16)
    gn_o[...] = proj(wgn_ref).astype(BF16)
    gg_o[...] = proj(wgg_ref).astype(BF16)


def _proj(x2, p, tm):
    n = x2.shape[0]
    wide = (NA_WIDTH,) * 7
    out_shapes = ([jax.ShapeDtypeStruct((n, w), BF16) for w in wide]
                  + [jax.ShapeDtypeStruct((n, GLA_WIDTH), F32)] * 2
                  + [jax.ShapeDtypeStruct((n, D_MODEL), BF16)] * 2)
    out_specs = [pl.BlockSpec((tm, s.shape[1]), lambda i: (i, 0)) for s in out_shapes]
    consts = [p["norm1_w"], p["w_q"], p["w_k"], p["w_v"], p["w_qg"], p["w_kg"], p["w_vg"], p["w_og"],
              p["w_lr"], p["w_gn"], p["w_gg"], p["bd"], p["qn_w"], p["kn_w"], p["w_a2"], p["b_a"]]
    return pl.pallas_call(
        _proj_kernel,
        grid=(n // tm,),
        in_specs=[pl.BlockSpec((tm, D_MODEL), lambda i: (i, 0))] + [_const_spec(c.shape) for c in consts],
        out_specs=out_specs,
        out_shape=out_shapes,
        scratch_shapes=[pltpu.VMEM((tm, D_MODEL), BF16)],
        compiler_params=pltpu.CompilerParams(dimension_semantics=("arbitrary",),
                                             vmem_limit_bytes=V7X_VMEM_LIMIT),
        name="proj",
    )(x2, *consts)


NA_QROWS_OPTIONS = (8, 16)
NA_HALO_TOK = (NA_ROWS // 2) * GRID_W
NA_SUB_ROWS = 4
NA_SUB_TOK = NA_SUB_ROWS * GRID_W
NA_SLAB_ROWS = NA_SUB_ROWS + NA_ROWS
NA_SLAB_TOK = NA_SLAB_ROWS * GRID_W
NA_LANES = 256
NA_GRP_HEADS = NA_LANES // NA_HEAD_DIM
NA_BLK = 32


def _na_bias_table(rpb):
    qc = np.arange(GRID_W)
    c0 = np.clip(qc - NA_COLS // 2, 0, GRID_W - NA_COLS)
    col_ok = (qc[None, :] >= c0[:, None]) & (qc[None, :] < c0[:, None] + NA_COLS)
    dc = np.clip(qc[None, :] - qc[:, None] + NA_COLS - 1, 0, 2 * NA_COLS - 2)
    sel = (dc[None] == np.arange(2 * NA_COLS - 1)[:, None, None]) & col_ok[None]
    t1 = jnp.einsum("hdc,cqk->hdqk", rpb.astype(F32), jnp.asarray(sel, F32), precision=lax.Precision.HIGHEST)
    t1 = jnp.where(jnp.asarray(col_ok)[None, None], t1, NEG)
    base = NA_ROWS - 1 - NA_ROWS // 2
    per_row = [t1[:, base - qi:base - qi + NA_SLAB_ROWS].transpose(0, 2, 1, 3) for qi in range(NA_SUB_ROWS)]
    tab = jnp.stack(per_row, axis=1)
    qi = np.arange(NA_SUB_ROWS)[:, None]
    kr = np.arange(NA_SLAB_ROWS)[None, :]
    first = [qi, np.full_like(qi, NA_ROWS // 2), np.zeros_like(qi)]
    row_ok = np.stack([(kr >= f) & (kr < f + NA_ROWS) for f in first])
    tab = jnp.where(jnp.asarray(row_ok)[:, None, :, None, :, None], tab[None], NEG)
    return tab.reshape(3, NA_HEADS, NA_SUB_TOK, NA_SLAB_TOK)


def _na_kernel(q_ref, kp_ref, kc_ref, kn_ref, vp_ref, vc_ref, vn_ref, bias_ref, o_ref,
               ku_s, vu_s, s_s, p_s, *, qrows):
    g = pl.program_id(1)
    last_g = pl.num_programs(1) - 1
    qtok = qrows * GRID_W
    n_sub = qrows // NA_SUB_ROWS
    ku_s[0:NA_HALO_TOK] = kp_ref[0, qtok - NA_HALO_TOK:qtok]
    ku_s[NA_HALO_TOK:NA_HALO_TOK + qtok] = kc_ref[0]
    ku_s[NA_HALO_TOK + qtok:qtok + 2 * NA_HALO_TOK] = kn_ref[0, 0:NA_HALO_TOK]
    vu_s[0:NA_HALO_TOK] = vp_ref[0, qtok - NA_HALO_TOK:qtok]
    vu_s[NA_HALO_TOK:NA_HALO_TOK + qtok] = vc_ref[0]
    vu_s[NA_HALO_TOK + qtok:qtok + 2 * NA_HALO_TOK] = vn_ref[0, 0:NA_HALO_TOK]

    variant = [0] * n_sub
    variant[0] = jnp.where(g == 0, 1, 0)
    variant[n_sub - 1] = jnp.where(g == last_g, 2, 0)

    lane_q = lax.broadcasted_iota(jnp.int32, (NA_SUB_TOK, NA_LANES), 1)
    unit = 0
    for j in range(n_sub):
        q_rows = slice(j * NA_SUB_TOK, (j + 1) * NA_SUB_TOK)
        k_rows = slice(j * NA_SUB_TOK, j * NA_SUB_TOK + NA_SLAB_TOK)
        for grp in range(NA_WIDTH // NA_LANES):
            lanes = slice(grp * NA_LANES, (grp + 1) * NA_LANES)
            q_grp = q_ref[0, q_rows, lanes]
            for hh in range(NA_GRP_HEADS):
                h = grp * NA_GRP_HEADS + hh
                slot = unit % 2
                unit += 1
                lo, hi = hh * NA_HEAD_DIM, (hh + 1) * NA_HEAD_DIM
                qm = jnp.where((lane_q >= lo) & (lane_q < hi), q_grp, jnp.zeros_like(q_grp))
                s_s[slot] = _dot_nt(qm, ku_s[k_rows, lanes])
                inv_l = []
                for rb in range(NA_SUB_TOK // NA_BLK):
                    blk = slice(rb * NA_BLK, (rb + 1) * NA_BLK)
                    s = s_s[slot, blk, :] + bias_ref[variant[j], h, blk, :]
                    p = jnp.exp(s - jnp.max(s, axis=-1, keepdims=True))
                    inv_l.append(1.0 / jnp.sum(p, axis=-1, keepdims=True))
                    p_s[slot, blk, :] = p.astype(BF16)
                o = _dot(p_s[slot], vu_s[k_rows, lanes])
                o_ref[0, q_rows, grp * NA_LANES + lo:grp * NA_LANES + hi] = (
                    o[:, lo:hi] * jnp.concatenate(inv_l, axis=0)).astype(BF16)


def _na(qn, kn, vn, bias, rows):
    b, t, _ = qn.shape
    qrows = max(q for q in NA_QROWS_OPTIONS if rows % q == 0)
    qtok = qrows * GRID_W
    ng = rows // qrows
    blk = (1, qtok, NA_WIDTH)
    cur = pl.BlockSpec(blk, lambda i, g: (i, g, 0))
    prev = pl.BlockSpec(blk, lambda i, g: (i, jnp.maximum(g - 1, 0), 0))
    nxt = pl.BlockSpec(blk, lambda i, g: (i, jnp.minimum(g + 1, ng - 1), 0))
    return pl.pallas_call(
        functools.partial(_na_kernel, qrows=qrows),
        grid=(b, ng),
        in_specs=[cur, prev, cur, nxt, prev, cur, nxt, _const_spec(bias.shape)],
        out_specs=cur,
        out_shape=jax.ShapeDtypeStruct((b, t, NA_WIDTH), BF16),
        scratch_shapes=[pltpu.VMEM((qtok + 2 * NA_HALO_TOK, NA_WIDTH), BF16),
                        pltpu.VMEM((qtok + 2 * NA_HALO_TOK, NA_WIDTH), BF16),
                        pltpu.VMEM((2, NA_SUB_TOK, NA_SLAB_TOK), F32),
                        pltpu.VMEM((2, NA_SUB_TOK, NA_SLAB_TOK), BF16)],
        compiler_params=pltpu.CompilerParams(dimension_semantics=("arbitrary", "arbitrary"),
                                             vmem_limit_bytes=V7X_VMEM_LIMIT),
        name="na",
    )(qn, kn, kn, kn, vn, vn, vn, bias)


def _gla_kernel(*refs, reverse, final, n_chunks, n_seq):
    if final:
        q_ref, k_ref, v_ref, b_ref, of_ref, og_ref, nw_ref, o_ref, st_s, tmp_s = refs
    else:
        q_ref, k_ref, v_ref, b_ref, o_ref, st_s, tmp_s = refs
    c_len = GLA_CHUNK

    @pl.when(pl.program_id(1) == 0)
    def _():
        st_s[...] = jnp.zeros_like(st_s)

    ii = lax.broadcasted_iota(jnp.int32, (c_len, c_len), 0)
    jj = lax.broadcasted_iota(jnp.int32, (c_len, c_len), 1)
    pair_ok = (jj > ii) if reverse else (jj <= ii)
    row_id = lax.broadcasted_iota(jnp.int32, (c_len, 1), 0)

    def emit(s, rows, hl, o):
        if final:
            o = o + of_ref[s, rows, hl]
            o = o * lax.rsqrt(jnp.mean(o * o, axis=-1, keepdims=True) + EPS) * nw_ref[...]
            og = og_ref[s, rows, hl].astype(F32)
            o_ref[s, rows, hl] = (o * (og * jax.nn.sigmoid(og))).astype(o_ref.dtype)
        else:
            o_ref[s, rows, hl] = o

    def chunk_body(c, carry, *, fast):
        cc = (n_chunks - 1 - c) if reverse else c
        r0 = pl.multiple_of(cc * c_len, c_len)
        rows = pl.ds(r0, c_len)
        for s, h in [(s, h) for s in range(n_seq) for h in range(GLA_HEADS)]:
            hl = slice(h * GLA_DK, (h + 1) * GLA_DK)
            q = q_ref[s, rows, hl].astype(F32)
            k = k_ref[s, rows, hl].astype(F32)
            v = v_ref[s, rows, hl]
            b = b_ref[s, rows, hl]
            tot = b[0:1] if reverse else b[c_len - 1:c_len]
            qf = q * jnp.exp(b)
            st = st_s[s, h]
            o_inter = _dot_nt(qf.astype(BF16), st.astype(BF16))
            if fast:
                kf = k * jnp.exp(-b)
                kt = kf * jnp.exp(tot)
                a = _dot_nt(qf.astype(BF16), kf.astype(BF16))
                a = jnp.where(pair_ok, a, 0.0)
                o_intra = _dot(a.astype(BF16), v)
            else:
                kt = k * jnp.exp(tot - b)
                tmp_s[0] = k
                tmp_s[1] = v.astype(F32)
                tmp_s[2] = b

                def key_step(j, acc):
                    kj = tmp_s[0, pl.ds(j, 1), :]
                    vj = tmp_s[1, pl.ds(j, 1), :]
                    bj = tmp_s[2, pl.ds(j, 1), :]
                    ok = (row_id < j) if reverse else (row_id >= j)
                    e = jnp.exp(jnp.where(ok, b - bj, -jnp.inf))
                    a_col = jnp.sum(q * kj * e, axis=-1, keepdims=True)
                    return acc + a_col * vj

                o_intra = lax.fori_loop(0, c_len, key_step, jnp.zeros((c_len, GLA_DV), F32))
            st_s[s, h] = st * jnp.exp(tot) + _dot_tn(v, kt.astype(BF16))
            emit(s, rows, hl, o_inter + o_intra)
        return carry

    fast_ok = jnp.min(b_ref[...]) >= -GLA_FAST_LIMIT

    @pl.when(fast_ok)
    def _():
        lax.fori_loop(0, n_chunks, functools.partial(chunk_body, fast=True), 0, unroll=True)

    @pl.when(jnp.logical_not(fast_ok))
    def _():
        lax.fori_loop(0, n_chunks, functools.partial(chunk_body, fast=False), 0)


def _gla(qg, kg, vg, bsum, tb, reverse, o_fwd=None, og=None, norm_w=None):
    b, t, _ = qg.shape
    nt = t // tb
    final = o_fwd is not None
    n_seq = 2 if b % 2 == 0 else 1
    blk = (n_seq, tb, GLA_WIDTH)
    if reverse:
        spec = pl.BlockSpec(blk, lambda i, j: (i, nt - 1 - j, 0))
    else:
        spec = pl.BlockSpec(blk, lambda i, j: (i, j, 0))
    args = [qg, kg, vg, bsum]
    in_specs = [spec] * 4
    if final:
        args += [o_fwd, og, norm_w]
        in_specs += [spec, spec, _const_spec(norm_w.shape)]
    return pl.pallas_call(
        functools.partial(_gla_kernel, reverse=reverse, final=final, n_chunks=tb // GLA_CHUNK, n_seq=n_seq),
        grid=(b // n_seq, nt),
        in_specs=in_specs,
        out_specs=spec,
        out_shape=jax.ShapeDtypeStruct((b, t, GLA_WIDTH), BF16 if final else F32),
        scratch_shapes=[pltpu.VMEM((n_seq, GLA_HEADS, GLA_DV, GLA_DK), F32),
                        pltpu.VMEM((3, GLA_CHUNK, GLA_DK), F32)],
        compiler_params=pltpu.CompilerParams(dimension_semantics=("arbitrary", "arbitrary"),
                                             vmem_limit_bytes=V7X_VMEM_LIMIT),
        name="gla_bwd" if reverse else "gla_fwd",
    )(*args)


def _merge_tile(ona_ref, og_ref, gn_ref, gg_ref, x_ref, wna_ref, wgla_ref, wout_ref, n2_ref):
    y_na = _dot(ona_ref[...], wna_ref[...])
    y_gla = _dot(og_ref[...], wgla_ref[...])
    mix = (jax.nn.sigmoid(gn_ref[...].astype(F32)) * y_na
           + jax.nn.sigmoid(gg_ref[...].astype(F32)) * y_gla)
    h = x_ref[...] + _dot(mix.astype(BF16), wout_ref[...])
    ms = jnp.mean(h * h, axis=-1, keepdims=True)
    return h, (h * lax.rsqrt(ms + EPS) * n2_ref[...]).astype(BF16)


def _gelu_tanh(x):
    return 0.5 * x * (1.0 + jnp.tanh(np.sqrt(2.0 / np.pi).astype(np.float32) * (x + 0.044715 * (x * x * x))))


def _ffn_tile(hx_s, y_o, wup_ref, cw_ref, cb_ref, wdn_ref, u_s):
    tm = y_o.shape[0]

    def up(c, slot):
        hx = hx_s[...]
        for part in range(2):
            u_s[slot][part][...] = _dot(hx, wup_ref[part * FFN_CHUNKS + c])

    def conv(u_ref, c):
        w = cw_ref[c]
        return (u_ref[HALO - 1:HALO - 1 + tm, :] * w[0:1] + u_ref[HALO:HALO + tm, :] * w[1:2]
                + u_ref[HALO + 1:HALO + 1 + tm, :] * w[2:3] + cb_ref[c])

    def gate(c, slot):
        return (_gelu_tanh(conv(u_s[slot][0], c)) * conv(u_s[slot][1], FFN_CHUNKS + c)).astype(BF16)

    def down(c, f):
        y_o[...] += _dot(f, wdn_ref[c])

    up(0, 0)

    def pair_step(i, carry):
        c = 2 * i
        f = gate(c, 0)
        up(c + 1, 1)
        down(c, f)
        f = gate(c + 1, 1)
        up(c + 2, 0)
        down(c + 1, f)
        return carry

    lax.fori_loop(0, FFN_CHUNKS // 2, pair_step, 0)
    down(FFN_CHUNKS - 1, gate(FFN_CHUNKS - 1, 0))


def _mix_ffn_kernel(ona_ref, og_ref, gn_ref, gg_ref, x_ref, wna_ref, wgla_ref, wout_ref, n2_ref,
                    wup_ref, cw_ref, cb_ref, wdn_ref, y_o, h_s, hn_s, hx_s, ua0_s, ub0_s, ua1_s, ub1_s,
                    *, tiles_per_seq):
    s = pl.program_id(0)
    n_tiles = pl.num_programs(0) - 1
    tm = y_o.shape[0]

    @pl.when(s == 0)
    def _():
        hn_s[...] = jnp.zeros_like(hn_s)
        y_o[...] = jnp.zeros_like(y_o)

    @pl.when(s < n_tiles)
    def _():
        h, hn = _merge_tile(ona_ref, og_ref, gn_ref, gg_ref, x_ref, wna_ref, wgla_ref, wout_ref, n2_ref)
        h_s[s % 2] = h
        hn_s[s % 3] = hn

    @pl.when(s > 0)
    def _():
        t = s - 1
        pos = t % tiles_per_seq
        zero = jnp.zeros((HALO, D_MODEL), BF16)
        hx_s[0:HALO] = jnp.where(pos > 0, hn_s[(t + 2) % 3, tm - HALO:tm, :], zero)
        hx_s[HALO:HALO + tm] = hn_s[t % 3]
        hx_s[HALO + tm:HALO + tm + HALO] = jnp.where(pos < tiles_per_seq - 1, hn_s[s % 3, 0:HALO, :], zero)
        y_o[...] = h_s[t % 2]
        _ffn_tile(hx_s, y_o, wup_ref, cw_ref, cb_ref, wdn_ref, ((ua0_s, ub0_s), (ua1_s, ub1_s)))


def _mix_ffn(o_na, o_g, gn, gg, x2, p, tm, t):
    n = x2.shape[0]
    n_tiles = n // tm
    tok = lambda w: pl.BlockSpec((tm, w), lambda s: (jnp.minimum(s, n_tiles - 1), 0))
    consts = [p["w_na_proj"], p["w_gla_proj"], p["w_out"], p["norm2_w"],
              p["w_up"], p["conv_w"], p["conv_b"], p["w_down"]]
    return pl.pallas_call(
        functools.partial(_mix_ffn_kernel, tiles_per_seq=t // tm),
        grid=(n_tiles + 1,),
        in_specs=[tok(NA_WIDTH), tok(GLA_WIDTH), tok(D_MODEL), tok(D_MODEL), tok(D_MODEL)]
        + [_const_spec(c.shape) for c in consts],
        out_specs=pl.BlockSpec((tm, D_MODEL), lambda s: (jnp.maximum(s - 1, 0), 0)),
        out_shape=jax.ShapeDtypeStruct((n, D_MODEL), F32),
        scratch_shapes=[pltpu.VMEM((2, tm, D_MODEL), F32), pltpu.VMEM((3, tm, D_MODEL), BF16),
                        pltpu.VMEM((tm + 2 * HALO, D_MODEL), BF16)]
        + [pltpu.VMEM((tm + 2 * HALO, FFN_COLS), F32)] * 4,
        compiler_params=pltpu.CompilerParams(dimension_semantics=("arbitrary",),
                                             vmem_limit_bytes=V7X_VMEM_LIMIT),
        name="mix_ffn",
    )(o_na, o_g, gn, gg, x2, *consts)


def _prepare_params(norm1_w, w_in, qn_w, kn_w, rpb, w_a2_f, b_a_f, w_a2_b, b_a_b, gla_norm_w,
                    w_na_proj, w_gla_proj, w_out, norm2_w, w_up, conv_w, conv_b, w_down):
    w_in = w_in.astype(BF16)
    sizes = (NA_WIDTH,) * 3 + (GLA_WIDTH,) * 4 + (GLA_GATE_RANK,) * 2 + (D_MODEL,) * 2
    offs = np.concatenate([[0], np.cumsum(sizes)])
    cols = [w_in[:, offs[i]:offs[i + 1]] for i in range(len(sizes))]
    w_lr = jnp.zeros((D_MODEL, 128), BF16).at[:, :2 * GLA_GATE_RANK].set(jnp.concatenate(cols[7:9], axis=1))
    w_a2 = jnp.zeros((128, 2 * GLA_WIDTH), F32)
    w_a2 = w_a2.at[:GLA_GATE_RANK, :GLA_WIDTH].set(w_a2_f)
    w_a2 = w_a2.at[GLA_GATE_RANK:2 * GLA_GATE_RANK, GLA_WIDTH:].set(w_a2_b)
    head = np.arange(NA_WIDTH) // NA_HEAD_DIM
    bd = jnp.asarray(head[:, None] == head[None, :], BF16)
    up = w_up.astype(BF16).reshape(D_MODEL, 2 * FFN_CHUNKS, FFN_COLS).transpose(1, 0, 2)
    cw = jnp.zeros((2 * FFN_CHUNKS, 8, FFN_COLS), F32).at[:, :3].set(
        conv_w.reshape(3, 2 * FFN_CHUNKS, FFN_COLS).transpose(1, 0, 2))
    return {
        "norm1_w": norm1_w.reshape(1, D_MODEL),
        "w_q": cols[0], "w_k": cols[1], "w_v": cols[2],
        "w_qg": cols[3], "w_kg": cols[4], "w_vg": cols[5], "w_og": cols[6],
        "w_lr": w_lr, "w_gn": cols[9], "w_gg": cols[10],
        "bd": bd,
        "qn_w": jnp.tile(qn_w, NA_HEADS).reshape(1, NA_WIDTH),
        "kn_w": jnp.tile(kn_w, NA_HEADS).reshape(1, NA_WIDTH),
        "w_a2": w_a2.astype(BF16),
        "b_a": jnp.concatenate([b_a_f, b_a_b]).reshape(1, 2 * GLA_WIDTH),
        "na_bias": _na_bias_table(rpb),
        "gla_norm_w": gla_norm_w.reshape(1, GLA_DV),
        "w_na_proj": w_na_proj.astype(BF16), "w_gla_proj": w_gla_proj.astype(BF16),
        "w_out": w_out.astype(BF16),
        "norm2_w": norm2_w.reshape(1, D_MODEL),
        "w_up": up, "conv_w": cw, "conv_b": conv_b.reshape(2 * FFN_CHUNKS, 1, FFN_COLS),
        "w_down": w_down.astype(BF16).reshape(FFN_CHUNKS, FFN_COLS, D_MODEL),
    }


def _trunk(x, p):
    b, t, _ = x.shape
    rows = t // GRID_W
    assert rows % min(NA_QROWS_OPTIONS) == 0 and rows >= NA_ROWS, (b, t)
    tm = 512
    gla_tb = 1024 if t % 1024 == 0 else 512
    n = b * t
    x2 = x.reshape(n, D_MODEL)
    qn, kn, vn, qg, kg, vg, og, bf, bb, gn, gg = _proj(x2, p, tm)
    r3 = lambda a: a.reshape(b, t, a.shape[-1])
    o_na = _na(r3(qn), r3(kn), r3(vn), p["na_bias"], rows)
    o_f = _gla(r3(qg), r3(kg), r3(vg), r3(bf), gla_tb, reverse=False)
    o_g = _gla(r3(qg), r3(kg), r3(vg), r3(bb), gla_tb, reverse=True,
               o_fwd=o_f, og=r3(og), norm_w=p["gla_norm_w"])
    y = _mix_ffn(o_na.reshape(n, NA_WIDTH), o_g.reshape(n, GLA_WIDTH), gn, gg, x2, p, tm, t)
    return y.reshape(b, t, D_MODEL)


def kernel(x_prompt, x_sample, norm1_w, w_in, qn_w, kn_w, rpb, w_a2_f, b_a_f, w_a2_b, b_a_b, gla_norm_w,
           w_na_proj, w_gla_proj, w_out, norm2_w, w_up, conv_w, conv_b, w_down):
    weights = (norm1_w, w_in, qn_w, kn_w, rpb, w_a2_f, b_a_f, w_a2_b, b_a_b, gla_norm_w,
               w_na_proj, w_gla_proj, w_out, norm2_w, w_up, conv_w, conv_b, w_down)
    assert all(w.shape[0] == 1 for w in weights), "one layer"
    p = _prepare_params(*[w[0] for w in weights])
    return (_trunk(x_prompt, p), _trunk(x_sample, p))
```
